```python
import math
import jax, jax.numpy as jnp
from jax import lax
import numpy as np

D_MODEL = 1024
BATCH = 8
SEQ = 4096
DEPTH = 1

CHUNK = 64
CONV_DIM = 512
CONV_WIDTH = 31
SB_HEADS = 8
SB_HEAD_DIM = 64
SB_DIM = SB_HEADS * SB_HEAD_DIM
Q_BLOCK = 128
N_BRANCH = 2
PEER_HEADS = 8
PEER_KEY_DIM = 256
PEER_HALF = PEER_KEY_DIM // 2
N_KEYS = 128
N_EXPERTS = N_KEYS * N_KEYS
PEER_TOPK = 16
TOKEN_BLOCK = 128
LN_EPS = 1e-5
DEEPNORM_ALPHA = (2.0 * DEPTH) ** 0.25
DEEPNORM_BETA = (8.0 * DEPTH) ** -0.25

COL_CONV = 0
COL_Q = COL_CONV + 2 * CONV_DIM
COL_K = COL_Q + SB_DIM
COL_V = COL_K + SB_DIM
COL_GATE = COL_V + SB_DIM
IN_COLS = COL_GATE + N_BRANCH * D_MODEL

kernel_name = "hybrid_conv_stickbreak_peer_deepnorm"


def layer_norm(x, g, b):
    xf = x.astype(jnp.float32)
    mu = jnp.mean(xf, axis=-1, keepdims=True)
    var = jnp.mean(jnp.square(xf - mu), axis=-1, keepdims=True)
    y = (xf - mu) * lax.rsqrt(var + LN_EPS) * g.astype(jnp.float32) + b.astype(jnp.float32)
    return y.astype(x.dtype)


def conformer_conv(h_glu, w_dw, b_dw, ln_g, ln_b, w_pw2, b_pw2):
    a, gate = jnp.split(h_glu, 2, axis=-1)
    h = a * jax.nn.sigmoid(gate)
    h = lax.conv_general_dilated(
        h, w_dw[:, None, :], window_strides=(1,),
        padding=[(CONV_WIDTH - 1, 0)],
        dimension_numbers=('NWC', 'WIO', 'NWC'),
        feature_group_count=CONV_DIM) + b_dw
    h = layer_norm(h, ln_g, ln_b)
    h = jax.nn.silu(h)
    return h @ w_pw2 + b_pw2


def stick_breaking_attention(q, k, v):
    B, H, S, dh = q.shape
    nb = S // Q_BLOCK
    qb = q.reshape(B, H, nb, Q_BLOCK, dh).transpose(2, 0, 1, 3, 4)
    kf = k.astype(jnp.float32)
    vf = v.astype(jnp.float32)
    key_pos = jnp.arange(S)
    scale = 1.0 / math.sqrt(dh)

    def block(args):
        qi, i = args
        z = jnp.einsum('bhqd,bhkd->bhqk', qi.astype(jnp.float32), kf) * scale
        q_pos = i * Q_BLOCK + jnp.arange(Q_BLOCK)
        mask = key_pos[None, :] < q_pos[:, None]
        log_beta = jax.nn.log_sigmoid(z)
        log_one_minus = jnp.where(mask, jax.nn.log_sigmoid(-z), 0.0)
        suffix = lax.cumsum(log_one_minus, axis=3, reverse=True) - log_one_minus
        att = jnp.where(mask, jnp.exp(log_beta + suffix), 0.0)
        return jnp.einsum('bhqk,bhkd->bhqd', att, vf)

    out = lax.map(block, (qb, jnp.arange(nb)))
    out = out.transpose(1, 0, 3, 2, 4).reshape(B, S, H * dh)
    return out.astype(q.dtype)


def peer(x, w_q, keys_1, keys_2, expert_u, expert_v):
    B, S, D = x.shape
    xt = x.reshape(-1, TOKEN_BLOCK, D)
    k1 = keys_1.astype(jnp.float32)
    k2 = keys_2.astype(jnp.float32)

    def block(xb):
        q = (xb @ w_q).astype(jnp.float32).reshape(TOKEN_BLOCK, PEER_HEADS, 2, PEER_HALF)
        s1 = jnp.einsum('thd,hkd->thk', q[:, :, 0], k1)
        s2 = jnp.einsum('thd,hkd->thk', q[:, :, 1], k2)
        v1, i1 = lax.top_k(s1, PEER_TOPK)
        v2, i2 = lax.top_k(s2, PEER_TOPK)
        cand_s = (v1[..., :, None] + v2[..., None, :]).reshape(TOKEN_BLOCK, PEER_HEADS, PEER_TOPK * PEER_TOPK)
        cand_i = (i1[..., :, None] * N_KEYS + i2[..., None, :]).reshape(TOKEN_BLOCK, PEER_HEADS, PEER_TOPK * PEER_TOPK)
        top_s, pos = lax.top_k(cand_s, PEER_TOPK)
        idx = jnp.take_along_axis(cand_i, pos, axis=-1)
        gate = jax.nn.softmax(top_s, axis=-1)
        u = expert_u[idx]
        vv = expert_v[idx]
        act = jax.nn.gelu(jnp.einsum('td,thkd->thk', xb, u).astype(jnp.float32), approximate=False)
        return jnp.einsum('thk,thkd->td', (gate * act).astype(xb.dtype), vv)

    out = lax.map(block, xt)
    return out.reshape(B, S, D)


def setup_inputs(seed: int = 0) -> dict:
    key = jax.random.key(seed)
    ks = jax.random.split(key, 24)
    L, D = DEPTH, D_MODEL
    f32 = jnp.float32
    nrm = lambda k, shape, s: jax.random.normal(k, shape, f32) * s

    x = jax.random.normal(ks[0], (BATCH, SEQ, D), f32)
    w_in = nrm(ks[1], (L, D, IN_COLS), D ** -0.5)
    col_scale = jnp.ones((IN_COLS,), f32).at[COL_V:COL_GATE].set(DEEPNORM_BETA)
    w_in = w_in * col_scale
    b_in = nrm(ks[2], (L, IN_COLS), 0.02)
    conv_w_dw = nrm(ks[3], (L, CONV_WIDTH, CONV_DIM), CONV_WIDTH ** -0.5)
    conv_b_dw = nrm(ks[4], (L, CONV_DIM), 0.02)
    conv_ln_g = 1.0 + nrm(ks[5], (L, CONV_DIM), 0.02)
    conv_ln_b = nrm(ks[6], (L, CONV_DIM), 0.02)
    conv_w_pw2 = nrm(ks[7], (L, CONV_DIM, D), DEEPNORM_BETA * CONV_DIM ** -0.5)
    conv_b_pw2 = nrm(ks[8], (L, D), 0.02)
    sb_w_o = nrm(ks[9], (L, SB_DIM, D), DEEPNORM_BETA * SB_DIM ** -0.5)
    w_out = nrm(ks[10], (L, D, D), DEEPNORM_BETA * D ** -0.5)
    b_out = nrm(ks[11], (L, D), 0.02)
    ln1_g = 1.0 + nrm(ks[12], (L, D), 0.02)
    ln1_b = nrm(ks[13], (L, D), 0.02)
    peer_w_q = nrm(ks[14], (L, D, PEER_HEADS * PEER_KEY_DIM), D ** -0.5)
    peer_keys_1 = nrm(ks[15], (L, PEER_HEADS, N_KEYS, PEER_HALF), PEER_HALF ** -0.5)
    peer_keys_2 = nrm(ks[16], (L, PEER_HEADS, N_KEYS, PEER_HALF), PEER_HALF ** -0.5)
    peer_u = nrm(ks[17], (L, N_EXPERTS, D), D ** -0.5)
    peer_v = nrm(ks[18], (L, N_EXPERTS, D), DEEPNORM_BETA)
    ln2_g = 1.0 + nrm(ks[19], (L, D), 0.02)
    ln2_b = nrm(ks[20], (L, D), 0.02)
    return {"x": x, "w_in": w_in, "b_in": b_in,
            "conv_w_dw": conv_w_dw, "conv_b_dw": conv_b_dw,
            "conv_ln_g": conv_ln_g, "conv_ln_b": conv_ln_b,
            "conv_w_pw2": conv_w_pw2, "conv_b_pw2": conv_b_pw2,
            "sb_w_o": sb_w_o, "w_out": w_out, "b_out": b_out,
            "ln1_g": ln1_g, "ln1_b": ln1_b,
            "peer_w_q": peer_w_q, "peer_keys_1": peer_keys_1, "peer_keys_2": peer_keys_2,
            "peer_u": peer_u, "peer_v": peer_v,
            "ln2_g": ln2_g, "ln2_b": ln2_b}


def reference(x, w_in, b_in, conv_w_dw, conv_b_dw, conv_ln_g, conv_ln_b,
              conv_w_pw2, conv_b_pw2, sb_w_o, w_out, b_out, ln1_g, ln1_b,
              peer_w_q, peer_keys_1, peer_keys_2, peer_u, peer_v, ln2_g, ln2_b):
    B, S, D = x.shape
    for l in range(DEPTH):
        h = x @ w_in[l] + b_in[l]
        h_conv = h[..., COL_CONV:COL_Q]
        q = h[..., COL_Q:COL_K].reshape(B, S, SB_HEADS, SB_HEAD_DIM).transpose(0, 2, 1, 3)
        k = h[..., COL_K:COL_V].reshape(B, S, SB_HEADS, SB_HEAD_DIM).transpose(0, 2, 1, 3)
        v = h[..., COL_V:COL_GATE].reshape(B, S, SB_HEADS, SB_HEAD_DIM).transpose(0, 2, 1, 3)
        g_conv = jax.nn.sigmoid(h[..., COL_GATE:COL_GATE + D])
        g_sb = jax.nn.sigmoid(h[..., COL_GATE + D:IN_COLS])

        y_conv = conformer_conv(h_conv, conv_w_dw[l], conv_b_dw[l], conv_ln_g[l], conv_ln_b[l],
                                conv_w_pw2[l], conv_b_pw2[l])
        y_sb = stick_breaking_attention(q, k, v) @ sb_w_o[l]
        mix = (g_conv * y_conv + g_sb * y_sb) @ w_out[l] + b_out[l]
        x = layer_norm(DEEPNORM_ALPHA * x + mix, ln1_g[l], ln1_b[l])

        y_peer = peer(x, peer_w_q[l], peer_keys_1[l], peer_keys_2[l], peer_u[l], peer_v[l])
        x = layer_norm(DEEPNORM_ALPHA * x + y_peer, ln2_g[l], ln2_b[l])
    return x
```

```python
import functools
import math

import jax
import jax.numpy as jnp
from jax import lax
from jax.experimental import pallas as pl
from jax.experimental.pallas import tpu as pltpu

F32 = jnp.float32
BF16 = jnp.bfloat16

DEPTH = 1
CONV_DIM = 512
CONV_WIDTH = 31
SB_HEADS = 8
SB_HEAD_DIM = 64
SB_DIM = SB_HEADS * SB_HEAD_DIM
PEER_HEADS = 8
PEER_HALF = 128
N_KEYS = 128
PEER_TOPK = 16
LN_EPS = 1e-5
DEEPNORM_ALPHA = (2.0 * DEPTH) ** 0.25

LANES = 128
HALO = 32
VMEM_LIMIT = 56 * 1024 * 1024


def _params(*sem):
    return pltpu.CompilerParams(dimension_semantics=sem, vmem_limit_bytes=VMEM_LIMIT)


def _layer_norm(x, g, b):
    mu = jnp.mean(x, axis=-1, keepdims=True)
    xc = x - mu
    var = jnp.mean(xc * xc, axis=-1, keepdims=True)
    return xc * lax.rsqrt(var + LN_EPS) * g + b


def _sigmoid(x):
    return 1.0 / (1.0 + jnp.exp(-x))


def _inproj_kernel(x_ref, w_ref, b_ref, conv_ref, qkv_ref, gate_ref, *, chunk):
    xb = x_ref[...].astype(BF16)
    col = 0
    for o_ref in (conv_ref, qkv_ref, gate_ref):
        width = o_ref.shape[1]
        for c in range(0, width, chunk):
            acc = jnp.dot(xb, w_ref[:, col + c:col + c + chunk], preferred_element_type=F32)
            o_ref[:, c:c + chunk] = (acc + b_ref[:, col + c:col + c + chunk]).astype(BF16)
        col += width


def _in_proj(x2, w_in, b_in, tm):
    t, d = x2.shape
    n = w_in.shape[1]
    widths = (2 * CONV_DIM, 3 * SB_DIM, n - 2 * CONV_DIM - 3 * SB_DIM)
    return pl.pallas_call(
        functools.partial(_inproj_kernel, chunk=512),
        grid=(t // tm,),
        in_specs=[pl.BlockSpec((tm, d), lambda i: (i, 0)),
                  pl.BlockSpec((d, n), lambda i: (0, 0)),
                  pl.BlockSpec((1, n), lambda i: (0, 0))],
        out_specs=[pl.BlockSpec((tm, w), lambda i: (i, 0)) for w in widths],
        out_shape=[jax.ShapeDtypeStruct((t, w), BF16) for w in widths],
        compiler_params=_params("parallel"),
        name="in_proj",
    )(x2, w_in, b_in)


def _conv_kernel(cur_ref, halo_ref, wdw_ref, bdw_ref, g_ref, b_ref, o_ref, buf_ref):
    ts = cur_ref.shape[0]

    def glu(ref):
        h = ref[...].astype(F32)
        return h[:, :CONV_DIM] * _sigmoid(h[:, CONV_DIM:])

    first = pl.program_id(1) == 0
    buf_ref[0:HALO, :] = jnp.where(first, 0.0, glu(halo_ref))
    buf_ref[HALO:HALO + ts, :] = glu(cur_ref)
    acc = jnp.zeros((ts, CONV_DIM), F32) + bdw_ref[...]
    for w in range(CONV_WIDTH):
        off = HALO - (CONV_WIDTH - 1) + w
        acc = acc + buf_ref[off:off + ts, :] * wdw_ref[w:w + 1, :]
    y = _layer_norm(acc, g_ref[...], b_ref[...])
    o_ref[...] = (y * _sigmoid(y)).astype(BF16)


def _conv_branch(hconv, w_dw, b_dw, ln_g, ln_b, batch, seq, ts):
    t = hconv.shape[0]
    ns = seq // ts
    per = ts // HALO
    vec = lambda a: a.reshape(1, -1)
    return pl.pallas_call(
        _conv_kernel,
        grid=(batch, ns),
        in_specs=[pl.BlockSpec((ts, 2 * CONV_DIM), lambda b, i: (b * ns + i, 0)),
                  pl.BlockSpec((HALO, 2 * CONV_DIM),
                               lambda b, i: (jnp.maximum((b * ns + i) * per - 1, 0), 0)),
                  pl.BlockSpec((CONV_WIDTH, CONV_DIM), lambda b, i: (0, 0)),
                  pl.BlockSpec((1, CONV_DIM), lambda b, i: (0, 0)),
                  pl.BlockSpec((1, CONV_DIM), lambda b, i: (0, 0)),
                  pl.BlockSpec((1, CONV_DIM), lambda b, i: (0, 0))],
        out_specs=pl.BlockSpec((ts, CONV_DIM), lambda b, i: (b * ns + i, 0)),
        out_shape=jax.ShapeDtypeStruct((t, CONV_DIM), BF16),
        scratch_shapes=[pltpu.VMEM((HALO + ts, CONV_DIM), F32)],
        compiler_params=_params("parallel", "parallel"),
        name="conv_branch",
    )(hconv, hconv, w_dw, vec(b_dw), vec(ln_g), vec(ln_b))


def _attn_kernel(q_ref, k_ref, v_ref, tri_ref, o_ref, *, blk):
    qi = pl.program_id(2)
    scale = 1.0 / math.sqrt(SB_HEAD_DIM)
    row = lax.broadcasted_iota(jnp.int32, (blk, blk), 0)
    col = lax.broadcasted_iota(jnp.int32, (blk, blk), 1)
    heads = LANES // SB_HEAD_DIM
    q = q_ref[...]
    tri = tri_ref[...]

    def body(step, state):
        j = qi - step
        start = pl.multiple_of(j * blk, blk)
        kblk = k_ref[pl.ds(start, blk), :]
        vblk = v_ref[pl.ds(start, blk), :]
        mask = (col - row) < (qi - j) * blk
        new_state = []
        for hh in range(heads):
            carry, acc = state[hh]
            lanes = slice(hh * SB_HEAD_DIM, (hh + 1) * SB_HEAD_DIM)
            z = lax.dot_general(q[:, lanes], kblk[:, lanes], (((1,), (1,)), ((), ())),
                                preferred_element_type=F32) * scale
            soft = jnp.log(1.0 + jnp.exp(-jnp.abs(z)))
            log_beta = jnp.minimum(z, 0.0) - soft
            log_om = jnp.where(mask, jnp.minimum(-z, 0.0) - soft, 0.0)
            hi = log_om.astype(BF16)
            lo = (log_om - hi.astype(F32)).astype(BF16)
            sums = jnp.dot(jnp.concatenate([hi, lo], axis=1), tri, preferred_element_type=F32)
            logit = log_beta + sums[:, :blk] + carry
            att = jnp.where(mask, jnp.exp(logit), 0.0)
            acc = acc + jnp.dot(att.astype(BF16), vblk[:, lanes], preferred_element_type=F32)
            new_state.append((carry + sums[:, blk:], acc))
        return tuple(new_state)

    init = tuple((jnp.zeros((blk, blk), F32), jnp.zeros((blk, SB_HEAD_DIM), F32))
                 for _ in range(heads))
    final = lax.fori_loop(0, qi + 1, body, init)
    o_ref[...] = jnp.concatenate([acc for _, acc in final], axis=1).astype(BF16)


def _suffix_matrix(blk):
    r = jnp.arange(2 * blk)[:, None] % blk
    c = jnp.arange(2 * blk)[None, :]
    return jnp.where((c >= blk) | (r > c), 1.0, 0.0).astype(BF16)


def _attention(qkv, batch, seq, blk):
    t = qkv.shape[0]
    nq = seq // blk
    pairs = SB_DIM // LANES
    return pl.pallas_call(
        functools.partial(_attn_kernel, blk=blk),
        grid=(batch, pairs, nq),
        in_specs=[pl.BlockSpec((blk, LANES), lambda b, p, i: (b * nq + i, p)),
                  pl.BlockSpec((seq, LANES), lambda b, p, i: (b, pairs + p)),
                  pl.BlockSpec((seq, LANES), lambda b, p, i: (b, 2 * pairs + p)),
                  pl.BlockSpec((2 * blk, 2 * blk), lambda b, p, i: (0, 0))],
        out_specs=pl.BlockSpec((blk, LANES), lambda b, p, i: (b * nq + i, p)),
        out_shape=jax.ShapeDtypeStruct((t, SB_DIM), BF16),
        compiler_params=_params("parallel", "parallel", "arbitrary"),
        name="stick_breaking_attn",
    )(qkv, qkv, qkv, _suffix_matrix(blk))


def _mix_kernel(x_ref, hc_ref, at_ref, gate_ref, wpw_ref, bpw_ref, wo_ref, wout_ref, bout_ref,
                g_ref, b_ref, x1_ref, x1t_ref):
    d = x_ref.shape[1]
    y_conv = jnp.dot(hc_ref[...], wpw_ref[...], preferred_element_type=F32) + bpw_ref[...]
    y_sb = jnp.dot(at_ref[...], wo_ref[...], preferred_element_type=F32)
    gates = gate_ref[...].astype(F32)
    m = _sigmoid(gates[:, :d]) * y_conv + _sigmoid(gates[:, d:]) * y_sb
    mix = jnp.dot(m.astype(BF16), wout_ref[...], preferred_element_type=F32) + bout_ref[...]
    x1 = _layer_norm(DEEPNORM_ALPHA * x_ref[...] + mix, g_ref[...], b_ref[...])
    x1_ref[...] = x1
    x1t_ref[...] = x1.T.astype(BF16)


def _mix(x2, hc, attn, gates, w_pw2, b_pw2, w_o, w_out, b_out, ln_g, ln_b, tm):
    t, d = x2.shape
    vec = lambda a: a.reshape(1, -1)
    full = lambda a: pl.BlockSpec(a.shape, lambda i: (0, 0))
    rows = lambda w: pl.BlockSpec((tm, w), lambda i: (i, 0))
    args = (x2, hc, attn, gates, w_pw2, vec(b_pw2), w_o, w_out, vec(b_out), vec(ln_g), vec(ln_b))
    return pl.pallas_call(
        _mix_kernel,
        grid=(t // tm,),
        in_specs=[rows(d), rows(CONV_DIM), rows(SB_DIM), rows(2 * d)] + [full(a) for a in args[4:]],
        out_specs=[rows(d), pl.BlockSpec((d, tm), lambda i: (0, i))],
        out_shape=[jax.ShapeDtypeStruct((t, d), F32), jax.ShapeDtypeStruct((d, t), BF16)],
        compiler_params=_params("parallel"),
        name="mix",
    )(*args)


def _top16(s):
    n_keys = s.shape[0]
    iota = lax.broadcasted_iota(jnp.int32, s.shape, 0)
    cur = s
    rank = jnp.full(s.shape, PEER_TOPK, jnp.int32)
    vals = []
    for r in range(PEER_TOPK):
        m = jnp.max(cur, axis=0, keepdims=True)
        idx = jnp.min(jnp.where(cur == m, iota, n_keys), axis=0, keepdims=True)
        sel = iota == idx
        rank = jnp.where(sel, r, rank)
        cur = jnp.where(sel, -jnp.inf, cur)
        vals.append(m)
    return jnp.concatenate(vals, axis=0), rank


def _staircase(v1, v2):
    k = PEER_TOPK
    iota = lax.broadcasted_iota(jnp.int32, v1.shape, 0)
    top = v1[0:1] + v2[0:1]
    length = jnp.zeros(v1.shape, jnp.int32)
    front = v1 + v2[0:1]
    denom = jnp.zeros_like(top)
    for _ in range(k):
        m = jnp.max(front, axis=0, keepdims=True)
        row = jnp.min(jnp.where(front == m, iota, k), axis=0, keepdims=True)
        sel = iota == row
        denom = denom + jnp.exp(m - top)
        length = length + sel.astype(jnp.int32)
        nxt = jnp.sum(jnp.where(sel, length, 0), axis=0, keepdims=True)
        v2_next = jnp.sum(jnp.where(iota == nxt, v2, 0.0), axis=0, keepdims=True)
        v2_next = jnp.where(nxt >= k, -jnp.inf, v2_next)
        front = jnp.where(sel, v1 + v2_next, front)
    return length, denom


def _select_kernel(xt_ref, wq_ref, k1_ref, k2_ref, r2_ref, e2_ref, la_ref, ca_ref, q_ref):
    q_ref[...] = jnp.dot(wq_ref[...], xt_ref[...], preferred_element_type=F32).astype(BF16)

    def head(h, _):
        base = pl.multiple_of(h * 2 * PEER_HALF, 2 * PEER_HALF)
        s1 = jnp.dot(k1_ref[h], q_ref[pl.ds(base, PEER_HALF), :], preferred_element_type=F32)
        s2 = jnp.dot(k2_ref[h], q_ref[pl.ds(base + PEER_HALF, PEER_HALF), :],
                     preferred_element_type=F32)
        v1, rank1 = _top16(s1)
        v2, rank2 = _top16(s2)
        length, denom = _staircase(v1, v2)
        la = jnp.zeros(s1.shape, jnp.int32)
        for i in range(PEER_TOPK):
            la = jnp.where(rank1 == i, length[i:i + 1], la)
        r2_ref[h] = rank2.astype(F32)
        e2_ref[h] = jnp.exp(s2 - v2[0:1])
        la_ref[h] = la.astype(F32)
        ca_ref[h] = jnp.exp(s1 - v1[0:1]) / denom
        return 0

    lax.fori_loop(0, PEER_HEADS, head, 0)


def _select(x1t, wq_t, keys1, keys2, ts):
    d, t = x1t.shape
    nq = wq_t.shape[0]
    out = jax.ShapeDtypeStruct((PEER_HEADS, N_KEYS, t), F32)
    out_spec = pl.BlockSpec((PEER_HEADS, N_KEYS, ts), lambda i: (0, 0, i))
    key_spec = pl.BlockSpec((PEER_HEADS, N_KEYS, PEER_HALF), lambda i: (0, 0, 0))
    return pl.pallas_call(
        _select_kernel,
        grid=(t // ts,),
        in_specs=[pl.BlockSpec((d, ts), lambda i: (0, i)),
                  pl.BlockSpec((nq, d), lambda i: (0, 0)),
                  key_spec, key_spec],
        out_specs=[out_spec] * 4,
        out_shape=[out] * 4,
        scratch_shapes=[pltpu.VMEM((nq, ts), BF16)],
        compiler_params=_params("parallel"),
        name="peer_select",
    )(x1t, wq_t, keys1, keys2)


def _gelu(x):
    return 0.5 * x * (1.0 + lax.erf(x * (1.0 / math.sqrt(2.0))))


def _experts_kernel(xt_ref, x1_ref, u_ref, vt_ref, r2_ref, e2_ref, la_ref, ca_ref, g_ref, b_ref,
                    o_ref, acc_ref, w_ref):
    j = pl.program_id(1)
    groups = u_ref.shape[0] // N_KEYS

    @pl.when(j == 0)
    def _():
        acc_ref[...] = jnp.zeros_like(acc_ref)

    act = jnp.dot(u_ref[...], xt_ref[...], preferred_element_type=F32)
    for ga in range(groups):
        a = j * groups + ga
        w = jnp.zeros((N_KEYS, xt_ref.shape[1]), F32)
        for h in range(PEER_HEADS):
            length = la_ref[h, pl.ds(a, 1), :]
            coef = ca_ref[h, pl.ds(a, 1), :]
            w = w + jnp.where(r2_ref[h] < length, e2_ref[h], 0.0) * coef
        rows = slice(ga * N_KEYS, (ga + 1) * N_KEYS)
        w_ref[rows, :] = (w * _gelu(act[rows, :])).astype(BF16)
    acc_ref[...] += jnp.dot(vt_ref[...], w_ref[...], preferred_element_type=F32)

    @pl.when(j == pl.num_programs(1) - 1)
    def _():
        y = acc_ref[...].T
        o_ref[...] = _layer_norm(DEEPNORM_ALPHA * x1_ref[...] + y, g_ref[...], b_ref[...])


def _experts(x1t, x1, u, v_t, sel, ln_g, ln_b, tb, eb):
    d, t = x1t.shape
    n_exp = u.shape[0]
    vec = lambda a: a.reshape(1, -1)
    sel_spec = pl.BlockSpec((PEER_HEADS, N_KEYS, tb), lambda i, j: (0, 0, i))
    return pl.pallas_call(
        _experts_kernel,
        grid=(t // tb, n_exp // eb),
        in_specs=[pl.BlockSpec((d, tb), lambda i, j: (0, i)),
                  pl.BlockSpec((tb, d), lambda i, j: (i, 0)),
                  pl.BlockSpec((eb, d), lambda i, j: (j, 0)),
                  pl.BlockSpec((d, eb), lambda i, j: (0, j)),
                  sel_spec, sel_spec, sel_spec, sel_spec,
                  pl.BlockSpec((1, d), lambda i, j: (0, 0)),
                  pl.BlockSpec((1, d), lambda i, j: (0, 0))],
        out_specs=pl.BlockSpec((tb, d), lambda i, j: (i, 0)),
        out_shape=jax.ShapeDtypeStruct((t, d), F32),
        scratch_shapes=[pltpu.VMEM((d, tb), F32), pltpu.VMEM((eb, tb), BF16)],
        compiler_params=_params("parallel", "arbitrary"),
        name="peer_experts",
    )(x1t, x1, u, v_t, *sel, vec(ln_g), vec(ln_b))


def _tile(total, want):
    return min(total, want)


def _layer(x2, batch, seq, w_in, b_in, conv_w_dw, conv_b_dw, conv_ln_g, conv_ln_b, conv_w_pw2,
           conv_b_pw2, sb_w_o, w_out, b_out, ln1_g, ln1_b, peer_w_q, keys1, keys2, peer_u, peer_v,
           ln2_g, ln2_b):
    t = x2.shape[0]
    hconv, qkv, gates = _in_proj(x2, w_in.astype(BF16), b_in.reshape(1, -1), _tile(t, 512))
    hc = _conv_branch(hconv, conv_w_dw, conv_b_dw, conv_ln_g, conv_ln_b, batch, seq,
                      _tile(seq, 512))
    attn = _attention(qkv, batch, seq, 128)
    x1, x1t = _mix(x2, hc, attn, gates, conv_w_pw2.astype(BF16), conv_b_pw2, sb_w_o.astype(BF16),
                   w_out.astype(BF16), b_out, ln1_g, ln1_b, _tile(t, 512))
    sel = _select(x1t, peer_w_q.T.astype(BF16), keys1.astype(BF16), keys2.astype(BF16),
                  _tile(t, 256))
    return _experts(x1t, x1, peer_u.astype(BF16), peer_v.T.astype(BF16), sel, ln2_g, ln2_b,
                    _tile(t, 512), _tile(peer_u.shape[0], 1024))


def kernel(x, w_in, b_in, conv_w_dw, conv_b_dw, conv_ln_g, conv_ln_b, conv_w_pw2, conv_b_pw2,
           sb_w_o, w_out, b_out, ln1_g, ln1_b, peer_w_q, peer_keys_1, peer_keys_2, peer_u, peer_v,
           ln2_g, ln2_b):
    batch, seq, d = x.shape
    x2 = x.reshape(batch * seq, d)
    for l in range(DEPTH):
        x2 = _layer(x2, batch, seq, w_in[l], b_in[l], conv_w_dw[l], conv_b_dw[l], conv_ln_g[l],
                    conv_ln_b[l], conv_w_pw2[l], conv_b_pw2[l], sb_w_o[l], w_out[l], b_out[l],
                    ln1_g[l], ln1_b[l], peer_w_q[l], peer_keys_1[l], peer_keys_2[l], peer_u[l],
                    peer_v[l], ln2_g[l], ln2_b[l])
    return x2.reshape(batch, seq, d)
```

```python
import functools
import math

import jax
import jax.numpy as jnp
from jax import lax
from jax.experimental import pallas as pl
from jax.experimental.pallas import tpu as pltpu

F32 = jnp.float32
BF16 = jnp.bfloat16

DEPTH = 1
CONV_DIM = 512
CONV_WIDTH = 31
SB_HEADS = 8
SB_HEAD_DIM = 64
SB_DIM = SB_HEADS * SB_HEAD_DIM
PEER_HEADS = 8
PEER_HALF = 128
N_KEYS = 128
PEER_TOPK = 16
LN_EPS = 1e-5
DEEPNORM_ALPHA = (2.0 * DEPTH) ** 0.25
EXP_UNDERFLOW = -104.0

LANES = 128
HALO = 32
VMEM_LIMIT = 56 * 1024 * 1024


def _params(*sem):
    return pltpu.CompilerParams(dimension_semantics=sem, vmem_limit_bytes=VMEM_LIMIT)


def _layer_norm(x, g, b):
    mu = jnp.mean(x, axis=-1, keepdims=True)
    xc = x - mu
    var = jnp.mean(xc * xc, axis=-1, keepdims=True)
    return xc * lax.rsqrt(var + LN_EPS) * g + b


def _sigmoid(x):
    return 1.0 / (1.0 + jnp.exp(-x))


def _inproj_kernel(x_ref, w_ref, b_ref, conv_ref, qkv_ref, gate_ref, *, chunk):
    xb = x_ref[...].astype(BF16)
    col = 0
    for o_ref in (conv_ref, qkv_ref, gate_ref):
        width = o_ref.shape[1]
        for c in range(0, width, chunk):
            acc = jnp.dot(xb, w_ref[:, col + c:col + c + chunk], preferred_element_type=F32)
            o_ref[:, c:c + chunk] = (acc + b_ref[:, col + c:col + c + chunk]).astype(BF16)
        col += width


def _in_proj(x2, w_in, b_in, tm):
    t, d = x2.shape
    n = w_in.shape[1]
    widths = (2 * CONV_DIM, 3 * SB_DIM, n - 2 * CONV_DIM - 3 * SB_DIM)
    return pl.pallas_call(
        functools.partial(_inproj_kernel, chunk=512),
        grid=(t // tm,),
        in_specs=[pl.BlockSpec((tm, d), lambda i: (i, 0)),
                  pl.BlockSpec((d, n), lambda i: (0, 0)),
                  pl.BlockSpec((1, n), lambda i: (0, 0))],
        out_specs=[pl.BlockSpec((tm, w), lambda i: (i, 0)) for w in widths],
        out_shape=[jax.ShapeDtypeStruct((t, w), BF16) for w in widths],
        compiler_params=_params("parallel"),
        name="in_proj",
    )(x2, w_in, b_in)


def _conv_kernel(cur_ref, halo_ref, wdw_ref, bdw_ref, g_ref, b_ref, o_ref, buf_ref):
    ts = cur_ref.shape[0]

    def glu(ref):
        h = ref[...].astype(F32)
        return h[:, :CONV_DIM] * _sigmoid(h[:, CONV_DIM:])

    first = pl.program_id(1) == 0
    buf_ref[0:HALO, :] = jnp.where(first, 0.0, glu(halo_ref))
    buf_ref[HALO:HALO + ts, :] = glu(cur_ref)
    acc = jnp.zeros((ts, CONV_DIM), F32) + bdw_ref[...]
    for w in range(CONV_WIDTH):
        off = HALO - (CONV_WIDTH - 1) + w
        acc = acc + buf_ref[off:off + ts, :] * wdw_ref[w:w + 1, :]
    y = _layer_norm(acc, g_ref[...], b_ref[...])
    o_ref[...] = (y * _sigmoid(y)).astype(BF16)


def _conv_branch(hconv, w_dw, b_dw, ln_g, ln_b, batch, seq, ts):
    t = hconv.shape[0]
    ns = seq // ts
    per = ts // HALO
    vec = lambda a: a.reshape(1, -1)
    return pl.pallas_call(
        _conv_kernel,
        grid=(batch, ns),
        in_specs=[pl.BlockSpec((ts, 2 * CONV_DIM), lambda b, i: (b * ns + i, 0)),
                  pl.BlockSpec((HALO, 2 * CONV_DIM),
                               lambda b, i: (jnp.maximum((b * ns + i) * per - 1, 0), 0)),
                  pl.BlockSpec((CONV_WIDTH, CONV_DIM), lambda b, i: (0, 0)),
                  pl.BlockSpec((1, CONV_DIM), lambda b, i: (0, 0)),
                  pl.BlockSpec((1, CONV_DIM), lambda b, i: (0, 0)),
                  pl.BlockSpec((1, CONV_DIM), lambda b, i: (0, 0))],
        out_specs=pl.BlockSpec((ts, CONV_DIM), lambda b, i: (b * ns + i, 0)),
        out_shape=jax.ShapeDtypeStruct((t, CONV_DIM), BF16),
        scratch_shapes=[pltpu.VMEM((HALO + ts, CONV_DIM), F32)],
        compiler_params=_params("parallel", "parallel"),
        name="conv_branch",
    )(hconv, hconv, w_dw, vec(b_dw), vec(ln_g), vec(ln_b))


def _attn_kernel(q_ref, k_ref, v_ref, tri_ref, o_ref, carry_ref, acc_ref, *, blk):
    qi = pl.program_id(1)
    scale = 1.0 / math.sqrt(SB_HEAD_DIM)
    row = lax.broadcasted_iota(jnp.int32, (blk, blk), 0)
    col = lax.broadcasted_iota(jnp.int32, (blk, blk), 1)
    causal = col < row
    tri = tri_ref[...]

    def tile(j, diagonal):
        start = pl.multiple_of(j * blk, blk)
        live = None
        for h in range(SB_HEADS):
            lanes = slice(h * SB_HEAD_DIM, (h + 1) * SB_HEAD_DIM)
            z = lax.dot_general(q_ref[:, lanes], k_ref[pl.ds(start, blk), lanes],
                                (((1,), (1,)), ((), ())), preferred_element_type=F32) * scale
            soft = jnp.log(1.0 + jnp.exp(-jnp.abs(z)))
            log_beta = jnp.minimum(z, 0.0) - soft
            log_om = jnp.minimum(-z, 0.0) - soft
            if diagonal:
                log_om = jnp.where(causal, log_om, 0.0)
            hi = log_om.astype(BF16)
            lo = (log_om - hi.astype(F32)).astype(BF16)
            sums = jnp.dot(jnp.concatenate([hi, lo], axis=1), tri, preferred_element_type=F32)
            logit = log_beta + sums[:, :blk]
            if diagonal:
                att = jnp.where(causal, jnp.exp(logit), 0.0)
                carry = sums[:, blk:]
            else:
                att = jnp.exp(logit + carry_ref[h])
                carry = carry_ref[h] + sums[:, blk:]
            carry_ref[h] = carry
            pv = jnp.dot(att.astype(BF16), v_ref[pl.ds(start, blk), lanes],
                         preferred_element_type=F32)
            acc_ref[:, lanes] = pv if diagonal else acc_ref[:, lanes] + pv
            live = carry if live is None else jnp.maximum(live, carry)
        return jnp.max(live)

    def cond(state):
        step, live = state
        return jnp.logical_and(step <= qi, live > EXP_UNDERFLOW)

    def body(state):
        step, _ = state
        return step + 1, tile(qi - step, False)

    lax.while_loop(cond, body, (jnp.int32(1), tile(qi, True)))
    o_ref[...] = acc_ref[...].astype(BF16)


def _suffix_matrix(blk):
    r = jnp.arange(2 * blk)[:, None] % blk
    c = jnp.arange(2 * blk)[None, :]
    return jnp.where((c >= blk) | (r > c), 1.0, 0.0).astype(BF16)


def _attention(qkv, batch, seq, blk):
    t = qkv.shape[0]
    nq = seq // blk
    return pl.pallas_call(
        functools.partial(_attn_kernel, blk=blk),
        grid=(batch, nq),
        in_specs=[pl.BlockSpec((blk, SB_DIM), lambda b, i: (b * nq + i, 0)),
                  pl.BlockSpec((seq, SB_DIM), lambda b, i: (b, 1)),
                  pl.BlockSpec((seq, SB_DIM), lambda b, i: (b, 2)),
                  pl.BlockSpec((2 * blk, 2 * blk), lambda b, i: (0, 0))],
        out_specs=pl.BlockSpec((blk, SB_DIM), lambda b, i: (b * nq + i, 0)),
        out_shape=jax.ShapeDtypeStruct((t, SB_DIM), BF16),
        scratch_shapes=[pltpu.VMEM((SB_HEADS, blk, blk), F32), pltpu.VMEM((blk, SB_DIM), F32)],
        compiler_params=_params("parallel", "arbitrary"),
        name="stick_breaking_attn",
    )(qkv, qkv, qkv, _suffix_matrix(blk))


def _mix_kernel(x_ref, hc_ref, at_ref, gate_ref, wpw_ref, bpw_ref, wo_ref, wout_ref, bout_ref,
                g_ref, b_ref, x1_ref, x1t_ref):
    d = x_ref.shape[1]
    y_conv = jnp.dot(hc_ref[...], wpw_ref[...], preferred_element_type=F32) + bpw_ref[...]
    y_sb = jnp.dot(at_ref[...], wo_ref[...], preferred_element_type=F32)
    gates = gate_ref[...].astype(F32)
    m = _sigmoid(gates[:, :d]) * y_conv + _sigmoid(gates[:, d:]) * y_sb
    mix = jnp.dot(m.astype(BF16), wout_ref[...], preferred_element_type=F32) + bout_ref[...]
    x1 = _layer_norm(DEEPNORM_ALPHA * x_ref[...] + mix, g_ref[...], b_ref[...])
    x1_ref[...] = x1
    x1t_ref[...] = x1.T.astype(BF16)


def _mix(x2, hc, attn, gates, w_pw2, b_pw2, w_o, w_out, b_out, ln_g, ln_b, tm):
    t, d = x2.shape
    vec = lambda a: a.reshape(1, -1)
    full = lambda a: pl.BlockSpec(a.shape, lambda i: (0, 0))
    rows = lambda w: pl.BlockSpec((tm, w), lambda i: (i, 0))
    args = (x2, hc, attn, gates, w_pw2, vec(b_pw2), w_o, w_out, vec(b_out), vec(ln_g), vec(ln_b))
    return pl.pallas_call(
        _mix_kernel,
        grid=(t // tm,),
        in_specs=[rows(d), rows(CONV_DIM), rows(SB_DIM), rows(2 * d)] + [full(a) for a in args[4:]],
        out_specs=[rows(d), pl.BlockSpec((d, tm), lambda i: (0, i))],
        out_shape=[jax.ShapeDtypeStruct((t, d), F32), jax.ShapeDtypeStruct((d, t), BF16)],
        compiler_params=_params("parallel"),
        name="mix",
    )(*args)


def _top16(s):
    n_keys = s.shape[0]
    iota = lax.broadcasted_iota(jnp.int32, s.shape, 0)
    cur = s
    rank = jnp.full(s.shape, PEER_TOPK, jnp.int32)
    vals = []
    for r in range(PEER_TOPK):
        m = jnp.max(cur, axis=0, keepdims=True)
        idx = jnp.min(jnp.where(cur == m, iota, n_keys), axis=0, keepdims=True)
        sel = iota == idx
        rank = jnp.where(sel, r, rank)
        cur = jnp.where(sel, -jnp.inf, cur)
        vals.append(m)
    return jnp.concatenate(vals, axis=0), rank


def _staircase(v1, v2):
    k = PEER_TOPK
    iota = lax.broadcasted_iota(jnp.int32, v1.shape, 0)
    top = v1[0:1] + v2[0:1]
    length = jnp.zeros(v1.shape, jnp.int32)
    front = v1 + v2[0:1]
    denom = jnp.zeros_like(top)
    for _ in range(k):
        m = jnp.max(front, axis=0, keepdims=True)
        row = jnp.min(jnp.where(front == m, iota, k), axis=0, keepdims=True)
        sel = iota == row
        denom = denom + jnp.exp(m - top)
        length = length + sel.astype(jnp.int32)
        nxt = jnp.sum(jnp.where(sel, length, 0), axis=0, keepdims=True)
        v2_next = jnp.sum(jnp.where(iota == nxt, v2, 0.0), axis=0, keepdims=True)
        v2_next = jnp.where(nxt >= k, -jnp.inf, v2_next)
        front = jnp.where(sel, v1 + v2_next, front)
    return length, denom


def _select_kernel(xt_ref, wq_ref, k1_ref, k2_ref, r2_ref, e2_ref, la_ref, ca_ref, q_ref):
    q_ref[...] = jnp.dot(wq_ref[...], xt_ref[...], preferred_element_type=F32).astype(BF16)

    def head(h, _):
        base = pl.multiple_of(h * 2 * PEER_HALF, 2 * PEER_HALF)
        s1 = jnp.dot(k1_ref[h], q_ref[pl.ds(base, PEER_HALF), :], preferred_element_type=F32)
        s2 = jnp.dot(k2_ref[h], q_ref[pl.ds(base + PEER_HALF, PEER_HALF), :],
                     preferred_element_type=F32)
        v1, rank1 = _top16(s1)
        v2, rank2 = _top16(s2)
        length, denom = _staircase(v1, v2)
        la = jnp.zeros(s1.shape, jnp.int32)
        for i in range(PEER_TOPK):
            la = jnp.where(rank1 == i, length[i:i + 1], la)
        r2_ref[h] = rank2.astype(F32)
        e2_ref[h] = jnp.exp(s2 - v2[0:1])
        la_ref[h] = la.astype(F32)
        ca_ref[h] = jnp.exp(s1 - v1[0:1]) / denom
        return 0

    lax.fori_loop(0, PEER_HEADS, head, 0)


def _select(x1t, wq_t, keys1, keys2, ts):
    d, t = x1t.shape
    nq = wq_t.shape[0]
    out = jax.ShapeDtypeStruct((PEER_HEADS, N_KEYS, t), F32)
    out_spec = pl.BlockSpec((PEER_HEADS, N_KEYS, ts), lambda i: (0, 0, i))
    key_spec = pl.BlockSpec((PEER_HEADS, N_KEYS, PEER_HALF), lambda i: (0, 0, 0))
    return pl.pallas_call(
        _select_kernel,
        grid=(t // ts,),
        in_specs=[pl.BlockSpec((d, ts), lambda i: (0, i)),
                  pl.BlockSpec((nq, d), lambda i: (0, 0)),
                  key_spec, key_spec],
        out_specs=[out_spec] * 4,
        out_shape=[out] * 4,
        scratch_shapes=[pltpu.VMEM((nq, ts), BF16)],
        compiler_params=_params("parallel"),
        name="peer_select",
    )(x1t, wq_t, keys1, keys2)


def _gelu(x):
    return 0.5 * x * (1.0 + lax.erf(x * (1.0 / math.sqrt(2.0))))


def _experts_kernel(xt_ref, x1_ref, u_ref, vt_ref, r2_ref, e2_ref, la_ref, ca_ref, g_ref, b_ref,
                    o_ref, acc_ref, w_ref):
    j = pl.program_id(1)
    groups = u_ref.shape[0] // N_KEYS

    @pl.when(j == 0)
    def _():
        acc_ref[...] = jnp.zeros_like(acc_ref)

    act = jnp.dot(u_ref[...], xt_ref[...], preferred_element_type=F32)
    for ga in range(groups):
        a = j * groups + ga
        w = jnp.zeros((N_KEYS, xt_ref.shape[1]), F32)
        for h in range(PEER_HEADS):
            length = la_ref[h, pl.ds(a, 1), :]
            coef = ca_ref[h, pl.ds(a, 1), :]
            w = w + jnp.where(r2_ref[h] < length, e2_ref[h], 0.0) * coef
        rows = slice(ga * N_KEYS, (ga + 1) * N_KEYS)
        w_ref[rows, :] = (w * _gelu(act[rows, :])).astype(BF16)
    acc_ref[...] += jnp.dot(vt_ref[...], w_ref[...], preferred_element_type=F32)

    @pl.when(j == pl.num_programs(1) - 1)
    def _():
        y = acc_ref[...].T
        o_ref[...] = _layer_norm(DEEPNORM_ALPHA * x1_ref[...] + y, g_ref[...], b_ref[...])


def _experts(x1t, x1, u, v_t, sel, ln_g, ln_b, tb, eb):
    d, t = x1t.shape
    n_exp = u.shape[0]
    vec = lambda a: a.reshape(1, -1)
    sel_spec = pl.BlockSpec((PEER_HEADS, N_KEYS, tb), lambda i, j: (0, 0, i))
    return pl.pallas_call(
        _experts_kernel,
        grid=(t // tb, n_exp // eb),
        in_specs=[pl.BlockSpec((d, tb), lambda i, j: (0, i)),
                  pl.BlockSpec((tb, d), lambda i, j: (i, 0)),
                  pl.BlockSpec((eb, d), lambda i, j: (j, 0)),
                  pl.BlockSpec((d, eb), lambda i, j: (0, j)),
                  sel_spec, sel_spec, sel_spec, sel_spec,
                  pl.BlockSpec((1, d), lambda i, j: (0, 0)),
                  pl.BlockSpec((1, d), lambda i, j: (0, 0))],
        out_specs=pl.BlockSpec((tb, d), lambda i, j: (i, 0)),
        out_shape=jax.ShapeDtypeStruct((t, d), F32),
        scratch_shapes=[pltpu.VMEM((d, tb), F32), pltpu.VMEM((eb, tb), BF16)],
        compiler_params=_params("parallel", "arbitrary"),
        name="peer_experts",
    )(x1t, x1, u, v_t, *sel, vec(ln_g), vec(ln_b))


def _tile(total, want):
    return min(total, want)


def _layer(x2, batch, seq, w_in, b_in, conv_w_dw, conv_b_dw, conv_ln_g, conv_ln_b, conv_w_pw2,
           conv_b_pw2, sb_w_o, w_out, b_out, ln1_g, ln1_b, peer_w_q, keys1, keys2, peer_u, peer_v,
           ln2_g, ln2_b):
    t = x2.shape[0]
    hconv, qkv, gates = _in_proj(x2, w_in.astype(BF16), b_in.reshape(1, -1), _tile(t, 512))
    hc = _conv_branch(hconv, conv_w_dw, conv_b_dw, conv_ln_g, conv_ln_b, batch, seq,
                      _tile(seq, 512))
    attn = _attention(qkv, batch, seq, 128)
    x1, x1t = _mix(x2, hc, attn, gates, conv_w_pw2.astype(BF16), conv_b_pw2, sb_w_o.astype(BF16),
                   w_out.astype(BF16), b_out, ln1_g, ln1_b, _tile(t, 512))
    sel = _select(x1t, peer_w_q.T.astype(BF16), keys1.astype(BF16), keys2.astype(BF16),
                  _tile(t, 256))
    return _experts(x1t, x1, peer_u.astype(BF16), peer_v.T.astype(BF16), sel, ln2_g, ln2_b,
                    _tile(t, 512), _tile(peer_u.shape[0], 1024))


def kernel(x, w_in, b_in, conv_w_dw, conv_b_dw, conv_ln_g, conv_ln_b, conv_w_pw2, conv_b_pw2,
           sb_w_o, w_out, b_out, ln1_g, ln1_b, peer_w_q, peer_keys_1, peer_keys_2, peer_u, peer_v,
           ln2_g, ln2_b):
    batch, seq, d = x.shape
    x2 = x.reshape(batch * seq, d)
    for l in range(DEPTH):
        x2 = _layer(x2, batch, seq, w_in[l], b_in[l], conv_w_dw[l], conv_b_dw[l], conv_ln_g[l],
                    conv_ln_b[l], conv_w_pw2[l], conv_b_pw2[l], sb_w_o[l], w_out[l], b_out[l],
                    ln1_g[l], ln1_b[l], peer_w_q[l], peer_keys_1[l], peer_keys_2[l], peer_u[l],
                    peer_v[l], ln2_g[l], ln2_b[l])
    return x2.reshape(batch, seq, d)
```

```python
import functools
import math

import jax
import jax.numpy as jnp
from jax import lax
from jax.experimental import pallas as pl
from jax.experimental.pallas import tpu as pltpu

F32 = jnp.float32
BF16 = jnp.bfloat16

DEPTH = 1
CONV_DIM = 512
CONV_WIDTH = 31
SB_HEADS = 8
SB_HEAD_DIM = 64
SB_DIM = SB_HEADS * SB_HEAD_DIM
PEER_HEADS = 8
PEER_HALF = 128
N_KEYS = 128
PEER_TOPK = 16
LN_EPS = 1e-5
DEEPNORM_ALPHA = (2.0 * DEPTH) ** 0.25
EXP_UNDERFLOW = -104.0

LANES = 128
F32_SUBLANES = 8
HALO = 32
VMEM_LIMIT = 56 * 1024 * 1024


def _params(*sem):
    return pltpu.CompilerParams(dimension_semantics=sem, vmem_limit_bytes=VMEM_LIMIT)


def _layer_norm(x, g, b):
    mu = jnp.mean(x, axis=-1, keepdims=True)
    xc = x - mu
    var = jnp.mean(xc * xc, axis=-1, keepdims=True)
    return xc * lax.rsqrt(var + LN_EPS) * g + b


def _sigmoid(x):
    return 1.0 / (1.0 + jnp.exp(-x))


def _inproj_kernel(x_ref, w_ref, b_ref, conv_ref, qkv_ref, gate_ref, *, chunk):
    xb = x_ref[...].astype(BF16)
    col = 0
    for o_ref in (conv_ref, qkv_ref, gate_ref):
        width = o_ref.shape[1]
        for c in range(0, width, chunk):
            acc = jnp.dot(xb, w_ref[:, col + c:col + c + chunk], preferred_element_type=F32)
            o_ref[:, c:c + chunk] = (acc + b_ref[:, col + c:col + c + chunk]).astype(BF16)
        col += width


def _in_proj(x2, w_in, b_in, tm):
    t, d = x2.shape
    n = w_in.shape[1]
    widths = (2 * CONV_DIM, 3 * SB_DIM, n - 2 * CONV_DIM - 3 * SB_DIM)
    return pl.pallas_call(
        functools.partial(_inproj_kernel, chunk=512),
        grid=(t // tm,),
        in_specs=[pl.BlockSpec((tm, d), lambda i: (i, 0)),
                  pl.BlockSpec((d, n), lambda i: (0, 0)),
                  pl.BlockSpec((1, n), lambda i: (0, 0))],
        out_specs=[pl.BlockSpec((tm, w), lambda i: (i, 0)) for w in widths],
        out_shape=[jax.ShapeDtypeStruct((t, w), BF16) for w in widths],
        compiler_params=_params("parallel"),
        name="in_proj",
    )(x2, w_in, b_in)


def _conv_kernel(cur_ref, halo_ref, wdw_ref, bdw_ref, g_ref, b_ref, o_ref, buf_ref):
    ts = cur_ref.shape[0]

    def glu(ref):
        h = ref[...].astype(F32)
        return h[:, :CONV_DIM] * _sigmoid(h[:, CONV_DIM:])

    first = pl.program_id(1) == 0
    buf_ref[0:HALO, :] = jnp.where(first, 0.0, glu(halo_ref))
    buf_ref[HALO:HALO + ts, :] = glu(cur_ref)
    acc = jnp.zeros((ts, CONV_DIM), F32) + bdw_ref[...]
    for w in range(CONV_WIDTH):
        off = HALO - (CONV_WIDTH - 1) + w
        acc = acc + buf_ref[off:off + ts, :] * wdw_ref[w:w + 1, :]
    y = _layer_norm(acc, g_ref[...], b_ref[...])
    o_ref[...] = (y * _sigmoid(y)).astype(BF16)


def _conv_branch(hconv, w_dw, b_dw, ln_g, ln_b, batch, seq, ts):
    t = hconv.shape[0]
    ns = seq // ts
    per = ts // HALO
    vec = lambda a: a.reshape(1, -1)
    return pl.pallas_call(
        _conv_kernel,
        grid=(batch, ns),
        in_specs=[pl.BlockSpec((ts, 2 * CONV_DIM), lambda b, i: (b * ns + i, 0)),
                  pl.BlockSpec((HALO, 2 * CONV_DIM),
                               lambda b, i: (jnp.maximum((b * ns + i) * per - 1, 0), 0)),
                  pl.BlockSpec((CONV_WIDTH, CONV_DIM), lambda b, i: (0, 0)),
                  pl.BlockSpec((1, CONV_DIM), lambda b, i: (0, 0)),
                  pl.BlockSpec((1, CONV_DIM), lambda b, i: (0, 0)),
                  pl.BlockSpec((1, CONV_DIM), lambda b, i: (0, 0))],
        out_specs=pl.BlockSpec((ts, CONV_DIM), lambda b, i: (b * ns + i, 0)),
        out_shape=jax.ShapeDtypeStruct((t, CONV_DIM), BF16),
        scratch_shapes=[pltpu.VMEM((HALO + ts, CONV_DIM), F32)],
        compiler_params=_params("parallel", "parallel"),
        name="conv_branch",
    )(hconv, hconv, w_dw, vec(b_dw), vec(ln_g), vec(ln_b))


def _attn_kernel(q_ref, k_ref, v_ref, tri_ref, o_ref, carry_ref, acc_ref, *, blk):
    qi = pl.program_id(1)
    scale = 1.0 / math.sqrt(SB_HEAD_DIM)
    row = lax.broadcasted_iota(jnp.int32, (blk, blk), 0)
    col = lax.broadcasted_iota(jnp.int32, (blk, blk), 1)
    causal = col < row
    tri = tri_ref[...]

    def tile(j, diagonal):
        start = pl.multiple_of(j * blk, blk)
        live = None
        for h in range(SB_HEADS):
            lanes = slice(h * SB_HEAD_DIM, (h + 1) * SB_HEAD_DIM)
            z = lax.dot_general(q_ref[:, lanes], k_ref[pl.ds(start, blk), lanes],
                                (((1,), (1,)), ((), ())), preferred_element_type=F32) * scale
            soft = jnp.log(1.0 + jnp.exp(-jnp.abs(z)))
            log_beta = jnp.minimum(z, 0.0) - soft
            log_om = jnp.minimum(-z, 0.0) - soft
            if diagonal:
                log_om = jnp.where(causal, log_om, 0.0)
            hi = log_om.astype(BF16)
            lo = (log_om - hi.astype(F32)).astype(BF16)
            sums = jnp.dot(jnp.concatenate([hi, lo], axis=1), tri, preferred_element_type=F32)
            logit = log_beta + sums[:, :blk]
            if diagonal:
                att = jnp.where(causal, jnp.exp(logit), 0.0)
                carry = sums[:, blk:]
            else:
                att = jnp.exp(logit + carry_ref[h])
                carry = carry_ref[h] + sums[:, blk:]
            carry_ref[h] = carry
            pv = jnp.dot(att.astype(BF16), v_ref[pl.ds(start, blk), lanes],
                         preferred_element_type=F32)
            acc_ref[:, lanes] = pv if diagonal else acc_ref[:, lanes] + pv
            live = carry if live is None else jnp.maximum(live, carry)
        return jnp.max(live)

    def cond(state):
        step, live = state
        return jnp.logical_and(step <= qi, live > EXP_UNDERFLOW)

    def body(state):
        step, _ = state
        return step + 1, tile(qi - step, False)

    lax.while_loop(cond, body, (jnp.int32(1), tile(qi, True)))
    o_ref[...] = acc_ref[...].astype(BF16)


def _suffix_matrix(blk):
    r = jnp.arange(2 * blk)[:, None] % blk
    c = jnp.arange(2 * blk)[None, :]
    return jnp.where((c >= blk) | (r > c), 1.0, 0.0).astype(BF16)


def _attention(qkv, batch, seq, blk):
    t = qkv.shape[0]
    nq = seq // blk
    return pl.pallas_call(
        functools.partial(_attn_kernel, blk=blk),
        grid=(batch, nq),
        in_specs=[pl.BlockSpec((blk, SB_DIM), lambda b, i: (b * nq + i, 0)),
                  pl.BlockSpec((seq, SB_DIM), lambda b, i: (b, 1)),
                  pl.BlockSpec((seq, SB_DIM), lambda b, i: (b, 2)),
                  pl.BlockSpec((2 * blk, 2 * blk), lambda b, i: (0, 0))],
        out_specs=pl.BlockSpec((blk, SB_DIM), lambda b, i: (b * nq + i, 0)),
        out_shape=jax.ShapeDtypeStruct((t, SB_DIM), BF16),
        scratch_shapes=[pltpu.VMEM((SB_HEADS, blk, blk), F32), pltpu.VMEM((blk, SB_DIM), F32)],
        compiler_params=_params("parallel", "arbitrary"),
        name="stick_breaking_attn",
    )(qkv, qkv, qkv, _suffix_matrix(blk))


def _mix_kernel(x_ref, hc_ref, at_ref, gate_ref, wpw_ref, bpw_ref, wo_ref, wout_ref, bout_ref,
                g_ref, b_ref, x1_ref, x1t_ref):
    d = x_ref.shape[1]
    y_conv = jnp.dot(hc_ref[...], wpw_ref[...], preferred_element_type=F32) + bpw_ref[...]
    y_sb = jnp.dot(at_ref[...], wo_ref[...], preferred_element_type=F32)
    gates = gate_ref[...].astype(F32)
    m = _sigmoid(gates[:, :d]) * y_conv + _sigmoid(gates[:, d:]) * y_sb
    mix = jnp.dot(m.astype(BF16), wout_ref[...], preferred_element_type=F32) + bout_ref[...]
    x1 = _layer_norm(DEEPNORM_ALPHA * x_ref[...] + mix, g_ref[...], b_ref[...])
    x1_ref[...] = x1
    x1t_ref[...] = x1.T.astype(BF16)


def _mix(x2, hc, attn, gates, w_pw2, b_pw2, w_o, w_out, b_out, ln_g, ln_b, tm):
    t, d = x2.shape
    vec = lambda a: a.reshape(1, -1)
    full = lambda a: pl.BlockSpec(a.shape, lambda i: (0, 0))
    rows = lambda w: pl.BlockSpec((tm, w), lambda i: (i, 0))
    args = (x2, hc, attn, gates, w_pw2, vec(b_pw2), w_o, w_out, vec(b_out), vec(ln_g), vec(ln_b))
    return pl.pallas_call(
        _mix_kernel,
        grid=(t // tm,),
        in_specs=[rows(d), rows(CONV_DIM), rows(SB_DIM), rows(2 * d)] + [full(a) for a in args[4:]],
        out_specs=[rows(d), pl.BlockSpec((d, tm), lambda i: (0, i))],
        out_shape=[jax.ShapeDtypeStruct((t, d), F32), jax.ShapeDtypeStruct((d, t), BF16)],
        compiler_params=_params("parallel"),
        name="mix",
    )(*args)


def _top16(s, break_ties):
    n_keys = s.shape[0]
    iota = lax.broadcasted_iota(jnp.int32, s.shape, 0).astype(F32)
    cur = s
    rank = jnp.full(s.shape, float(PEER_TOPK), F32)
    vals = []
    for r in range(PEER_TOPK):
        m = jnp.max(cur, axis=0, keepdims=True)
        sel = cur == m
        if break_ties:
            idx = jnp.min(jnp.where(sel, iota, float(n_keys)), axis=0, keepdims=True)
            sel = iota == idx
        rank = jnp.where(sel, float(r), rank)
        cur = jnp.where(sel, -jnp.inf, cur)
        vals.append(m)
    return jnp.concatenate(vals, axis=0), rank


def _ranked(s, v_ref, rank_ref):
    v, rank = _top16(s, False)
    v_ref[...] = v
    rank_ref[...] = rank
    ranked = jnp.sum(jnp.where(rank < PEER_TOPK, 1.0, 0.0), axis=0, keepdims=True)

    @pl.when(jnp.max(ranked) > PEER_TOPK)
    def _():
        v, rank = _top16(s, True)
        v_ref[...] = v
        rank_ref[...] = rank

    return v_ref[...], rank_ref[...]


def _staircase(v1, v2):
    k = float(PEER_TOPK)
    iota = lax.broadcasted_iota(jnp.int32, v1.shape, 0).astype(F32)
    top = v1[0:1] + v2[0:1]
    length = jnp.zeros(v1.shape, F32)
    front = v1 + v2[0:1]
    denom = jnp.zeros_like(top)
    for _ in range(PEER_TOPK):
        m = jnp.max(front, axis=0, keepdims=True)
        row = jnp.min(jnp.where(front == m, iota, k), axis=0, keepdims=True)
        sel = iota == row
        denom = denom + jnp.exp(m - top)
        length = jnp.where(sel, length + 1.0, length)
        nxt = jnp.sum(jnp.where(sel, length, 0.0), axis=0, keepdims=True)
        v2_next = jnp.sum(jnp.where(iota == nxt, v2, 0.0), axis=0, keepdims=True)
        v2_next = jnp.where(nxt >= k, -jnp.inf, v2_next)
        front = jnp.where(sel, v1 + v2_next, front)
    return length, denom


def _pack_pair(x):
    bits = pltpu.bitcast(x.astype(BF16).astype(F32), jnp.uint32)
    return bits | (bits >> 16)


def _select_kernel(xt_ref, wq_ref, k1_ref, k2_ref, r2_ref, e2_ref, la_ref, ca_ref, q_ref,
                   v1_ref, v2_ref, rank1_ref, rank2_ref):
    q_ref[...] = jnp.dot(wq_ref[...], xt_ref[...], preferred_element_type=F32).astype(BF16)

    def head(h, _):
        base = pl.multiple_of(h * 2 * PEER_HALF, 2 * PEER_HALF)
        s1 = jnp.dot(k1_ref[h], q_ref[pl.ds(base, PEER_HALF), :], preferred_element_type=F32)
        s2 = jnp.dot(k2_ref[h], q_ref[pl.ds(base + PEER_HALF, PEER_HALF), :],
                     preferred_element_type=F32)
        v1, rank1 = _ranked(s1, v1_ref, rank1_ref)
        v2, rank2 = _ranked(s2, v2_ref, rank2_ref)
        length, denom = _staircase(v1, v2)
        la = jnp.zeros(s1.shape, F32)
        for i in range(PEER_TOPK):
            la = jnp.where(rank1 == float(i), length[i:i + 1], la)
        r2_ref[h] = rank2.astype(BF16)
        e2_ref[h] = jnp.exp(s2 - v2[0:1]).astype(BF16)
        la_ref[h] = _pack_pair(la)
        ca_ref[h] = _pack_pair(jnp.exp(s1 - v1[0:1]) / denom)
        return 0

    lax.fori_loop(0, PEER_HEADS, head, 0)


def _select(x1t, wq_t, keys1, keys2, ts):
    d, t = x1t.shape
    nq = wq_t.shape[0]
    out = lambda dt: jax.ShapeDtypeStruct((PEER_HEADS, N_KEYS, t), dt)
    out_spec = pl.BlockSpec((PEER_HEADS, N_KEYS, ts), lambda i: (0, 0, i))
    key_spec = pl.BlockSpec((PEER_HEADS, N_KEYS, PEER_HALF), lambda i: (0, 0, 0))
    return pl.pallas_call(
        _select_kernel,
        grid=(t // ts,),
        in_specs=[pl.BlockSpec((d, ts), lambda i: (0, i)),
                  pl.BlockSpec((nq, d), lambda i: (0, 0)),
                  key_spec, key_spec],
        out_specs=[out_spec] * 4,
        out_shape=[out(BF16), out(BF16), out(jnp.uint32), out(jnp.uint32)],
        scratch_shapes=[pltpu.VMEM((nq, ts), BF16),
                        pltpu.VMEM((PEER_TOPK, ts), F32), pltpu.VMEM((PEER_TOPK, ts), F32),
                        pltpu.VMEM((N_KEYS, ts), F32), pltpu.VMEM((N_KEYS, ts), F32)],
        compiler_params=_params("parallel"),
        name="peer_select",
    )(x1t, wq_t, keys1, keys2)


def _gelu(x):
    return x * (0.5 * (1.0 + lax.erf(x * (1.0 / math.sqrt(2.0)))))


def _row_bf16(ref, h, a, lanes):
    row = ref[h, pl.ds(a, 1), lanes]
    sub = pltpu.bitcast(jnp.broadcast_to(row, (F32_SUBLANES, row.shape[1])), BF16)
    return jnp.concatenate([sub] * (N_KEYS // sub.shape[0]), axis=0)


def _experts_kernel(xt_ref, x1_ref, u_ref, vt_ref, r2_ref, e2_ref, la_ref, ca_ref, g_ref, b_ref,
                    o_ref, acc_ref, w_ref, *, halves, chunk):
    j = pl.program_id(1)
    eb, tb = u_ref.shape[0], xt_ref.shape[1]
    groups = eb // N_KEYS
    sub = eb // halves

    @pl.when(j == 0)
    def _():
        acc_ref[...] = jnp.zeros_like(acc_ref)

    xt = xt_ref[...]
    zero = jnp.zeros((N_KEYS, chunk), BF16)
    for s in range(halves):
        act = jnp.dot(u_ref[s * sub:(s + 1) * sub, :], xt, preferred_element_type=F32)
        for ga in range(sub // N_KEYS):
            a = j * groups + s * (sub // N_KEYS) + ga
            rows = slice(ga * N_KEYS, (ga + 1) * N_KEYS)
            for c in range(tb // chunk):
                lanes = slice(c * chunk, (c + 1) * chunk)
                w = None
                for h in range(PEER_HEADS):
                    keep = r2_ref[h, :, lanes] < _row_bf16(la_ref, h, a, lanes)
                    term = (jnp.where(keep, e2_ref[h, :, lanes], zero)
                            * _row_bf16(ca_ref, h, a, lanes))
                    w = term if w is None else w + term
                w_ref[s * sub + ga * N_KEYS:s * sub + (ga + 1) * N_KEYS, lanes] = (
                    w * _gelu(act[rows, lanes].astype(BF16)))
    acc_ref[...] += jnp.dot(vt_ref[...], w_ref[...], preferred_element_type=F32)

    @pl.when(j == pl.num_programs(1) - 1)
    def _():
        y = acc_ref[...].T
        o_ref[...] = _layer_norm(DEEPNORM_ALPHA * x1_ref[...] + y, g_ref[...], b_ref[...])


def _experts(x1t, x1, u, v_t, sel, ln_g, ln_b, tb, eb):
    d, t = x1t.shape
    n_exp = u.shape[0]
    vec = lambda a: a.reshape(1, -1)
    sel_spec = pl.BlockSpec((PEER_HEADS, N_KEYS, tb), lambda i, j: (0, 0, i))
    return pl.pallas_call(
        functools.partial(_experts_kernel, halves=2, chunk=min(tb, 256)),
        grid=(t // tb, n_exp // eb),
        in_specs=[pl.BlockSpec((d, tb), lambda i, j: (0, i)),
                  pl.BlockSpec((tb, d), lambda i, j: (i, 0)),
                  pl.BlockSpec((eb, d), lambda i, j: (j, 0)),
                  pl.BlockSpec((d, eb), lambda i, j: (0, j)),
                  sel_spec, sel_spec, sel_spec, sel_spec,
                  pl.BlockSpec((1, d), lambda i, j: (0, 0)),
                  pl.BlockSpec((1, d), lambda i, j: (0, 0))],
        out_specs=pl.BlockSpec((tb, d), lambda i, j: (i, 0)),
        out_shape=jax.ShapeDtypeStruct((t, d), F32),
        scratch_shapes=[pltpu.VMEM((d, tb), F32), pltpu.VMEM((eb, tb), BF16)],
        compiler_params=_params("parallel", "arbitrary"),
        name="peer_experts",
    )(x1t, x1, u, v_t, *sel, vec(ln_g), vec(ln_b))


def _tile(total, want):
    return min(total, want)


def _layer(x2, batch, seq, w_in, b_in, conv_w_dw, conv_b_dw, conv_ln_g, conv_ln_b, conv_w_pw2,
           conv_b_pw2, sb_w_o, w_out, b_out, ln1_g, ln1_b, peer_w_q, keys1, keys2, peer_u, peer_v,
           ln2_g, ln2_b):
    t = x2.shape[0]
    hconv, qkv, gates = _in_proj(x2, w_in.astype(BF16), b_in.reshape(1, -1), _tile(t, 512))
    hc = _conv_branch(hconv, conv_w_dw, conv_b_dw, conv_ln_g, conv_ln_b, batch, seq,
                      _tile(seq, 512))
    attn = _attention(qkv, batch, seq, 128)
    x1, x1t = _mix(x2, hc, attn, gates, conv_w_pw2.astype(BF16), conv_b_pw2, sb_w_o.astype(BF16),
                   w_out.astype(BF16), b_out, ln1_g, ln1_b, _tile(t, 512))
    sel = _select(x1t, peer_w_q.T.astype(BF16), keys1.astype(BF16), keys2.astype(BF16),
                  _tile(t, 256))
    return _experts(x1t, x1, peer_u.astype(BF16), peer_v.T.astype(BF16), sel, ln2_g, ln2_b,
                    _tile(t, 512), _tile(peer_u.shape[0], 1024))


def kernel(x, w_in, b_in, conv_w_dw, conv_b_dw, conv_ln_g, conv_ln_b, conv_w_pw2, conv_b_pw2,
           sb_w_o, w_out, b_out, ln1_g, ln1_b, peer_w_q, peer_keys_1, peer_keys_2, peer_u, peer_v,
           ln2_g, ln2_b):
    batch, seq, d = x.shape
    x2 = x.reshape(batch * seq, d)
    for l in range(DEPTH):
        x2 = _layer(x2, batch, seq, w_in[l], b_in[l], conv_w_dw[l], conv_b_dw[l], conv_ln_g[l],
                    conv_ln_b[l], conv_w_pw2[l], conv_b_pw2[l], sb_w_o[l], w_out[l], b_out[l],
                    ln1_g[l], ln1_b[l], peer_w_q[l], peer_keys_1[l], peer_keys_2[l], peer_u[l],
                    peer_v[l], ln2_g[l], ln2_b[l])
    return x2.reshape(batch, seq, d)
```

```python
import functools
import math

import jax
import jax.numpy as jnp
from jax import lax
from jax.experimental import pallas as pl
from jax.experimental.pallas import tpu as pltpu

F32 = jnp.float32
BF16 = jnp.bfloat16

DEPTH = 1
CONV_DIM = 512
CONV_WIDTH = 31
SB_HEADS = 8
SB_HEAD_DIM = 64
SB_DIM = SB_HEADS * SB_HEAD_DIM
PEER_HEADS = 8
PEER_HALF = 128
N_KEYS = 128
PEER_TOPK = 16
LN_EPS = 1e-5
DEEPNORM_ALPHA = (2.0 * DEPTH) ** 0.25
EXP_UNDERFLOW = -104.0

LANES = 128
F32_SUBLANES = 8
HALO = 32
VMEM_LIMIT = 56 * 1024 * 1024


def _params(*sem):
    return pltpu.CompilerParams(dimension_semantics=sem, vmem_limit_bytes=VMEM_LIMIT)


def _layer_norm(x, g, b):
    mu = jnp.mean(x, axis=-1, keepdims=True)
    xc = x - mu
    var = jnp.mean(xc * xc, axis=-1, keepdims=True)
    return xc * lax.rsqrt(var + LN_EPS) * g + b


def _sigmoid(x):
    return 1.0 / (1.0 + jnp.exp(-x))


def _inproj_kernel(x_ref, w_ref, b_ref, conv_ref, qkv_ref, gate_ref, *, chunk):
    xb = x_ref[...].astype(BF16)
    col = 0
    for o_ref in (conv_ref, qkv_ref, gate_ref):
        width = o_ref.shape[1]
        for c in range(0, width, chunk):
            acc = jnp.dot(xb, w_ref[:, col + c:col + c + chunk], preferred_element_type=F32)
            o_ref[:, c:c + chunk] = (acc + b_ref[:, col + c:col + c + chunk]).astype(BF16)
        col += width


def _in_proj(x2, w_in, b_in, tm):
    t, d = x2.shape
    n = w_in.shape[1]
    widths = (2 * CONV_DIM, 3 * SB_DIM, n - 2 * CONV_DIM - 3 * SB_DIM)
    return pl.pallas_call(
        functools.partial(_inproj_kernel, chunk=512),
        grid=(t // tm,),
        in_specs=[pl.BlockSpec((tm, d), lambda i: (i, 0)),
                  pl.BlockSpec((d, n), lambda i: (0, 0)),
                  pl.BlockSpec((1, n), lambda i: (0, 0))],
        out_specs=[pl.BlockSpec((tm, w), lambda i: (i, 0)) for w in widths],
        out_shape=[jax.ShapeDtypeStruct((t, w), BF16) for w in widths],
        compiler_params=_params("parallel"),
        name="in_proj",
    )(x2, w_in, b_in)


def _conv_kernel(cur_ref, halo_ref, wdw_ref, bdw_ref, g_ref, b_ref, o_ref, buf_ref):
    ts = cur_ref.shape[0]

    def glu(ref):
        h = ref[...].astype(F32)
        return h[:, :CONV_DIM] * _sigmoid(h[:, CONV_DIM:])

    first = pl.program_id(1) == 0
    buf_ref[0:HALO, :] = jnp.where(first, 0.0, glu(halo_ref))
    buf_ref[HALO:HALO + ts, :] = glu(cur_ref)
    acc = jnp.zeros((ts, CONV_DIM), F32) + bdw_ref[...]
    for w in range(CONV_WIDTH):
        off = HALO - (CONV_WIDTH - 1) + w
        acc = acc + buf_ref[off:off + ts, :] * wdw_ref[w:w + 1, :]
    y = _layer_norm(acc, g_ref[...], b_ref[...])
    o_ref[...] = (y * _sigmoid(y)).astype(BF16)


def _conv_branch(hconv, w_dw, b_dw, ln_g, ln_b, batch, seq, ts):
    t = hconv.shape[0]
    ns = seq // ts
    per = ts // HALO
    vec = lambda a: a.reshape(1, -1)
    return pl.pallas_call(
        _conv_kernel,
        grid=(batch, ns),
        in_specs=[pl.BlockSpec((ts, 2 * CONV_DIM), lambda b, i: (b * ns + i, 0)),
                  pl.BlockSpec((HALO, 2 * CONV_DIM),
                               lambda b, i: (jnp.maximum((b * ns + i) * per - 1, 0), 0)),
                  pl.BlockSpec((CONV_WIDTH, CONV_DIM), lambda b, i: (0, 0)),
                  pl.BlockSpec((1, CONV_DIM), lambda b, i: (0, 0)),
                  pl.BlockSpec((1, CONV_DIM), lambda b, i: (0, 0)),
                  pl.BlockSpec((1, CONV_DIM), lambda b, i: (0, 0))],
        out_specs=pl.BlockSpec((ts, CONV_DIM), lambda b, i: (b * ns + i, 0)),
        out_shape=jax.ShapeDtypeStruct((t, CONV_DIM), BF16),
        scratch_shapes=[pltpu.VMEM((HALO + ts, CONV_DIM), F32)],
        compiler_params=_params("parallel", "parallel"),
        name="conv_branch",
    )(hconv, hconv, w_dw, vec(b_dw), vec(ln_g), vec(ln_b))


def _attn_kernel(q_ref, k_ref, v_ref, tri_ref, o_ref, carry_ref, acc_ref, *, blk):
    qi = pl.program_id(1)
    scale = 1.0 / math.sqrt(SB_HEAD_DIM)
    row = lax.broadcasted_iota(jnp.int32, (blk, blk), 0)
    col = lax.broadcasted_iota(jnp.int32, (blk, blk), 1)
    causal = col < row
    tri = tri_ref[...]

    def tile(j, diagonal):
        start = pl.multiple_of(j * blk, blk)
        live = None
        for h in range(SB_HEADS):
            lanes = slice(h * SB_HEAD_DIM, (h + 1) * SB_HEAD_DIM)
            z = lax.dot_general(q_ref[:, lanes], k_ref[pl.ds(start, blk), lanes],
                                (((1,), (1,)), ((), ())), preferred_element_type=F32) * scale
            soft = jnp.log(1.0 + jnp.exp(-jnp.abs(z)))
            log_beta = jnp.minimum(z, 0.0) - soft
            log_om = jnp.minimum(-z, 0.0) - soft
            if diagonal:
                log_om = jnp.where(causal, log_om, 0.0)
            hi = log_om.astype(BF16)
            lo = (log_om - hi.astype(F32)).astype(BF16)
            sums = jnp.dot(jnp.concatenate([hi, lo], axis=1), tri, preferred_element_type=F32)
            logit = log_beta + sums[:, :blk]
            if diagonal:
                att = jnp.where(causal, jnp.exp(logit), 0.0)
                carry = sums[:, blk:]
            else:
                att = jnp.exp(logit + carry_ref[h])
                carry = carry_ref[h] + sums[:, blk:]
            carry_ref[h] = carry
            pv = jnp.dot(att.astype(BF16), v_ref[pl.ds(start, blk), lanes],
                         preferred_element_type=F32)
            acc_ref[:, lanes] = pv if diagonal else acc_ref[:, lanes] + pv
            live = carry if live is None else jnp.maximum(live, carry)
        return jnp.max(live)

    def cond(state):
        step, live = state
        return jnp.logical_and(step <= qi, live > EXP_UNDERFLOW)

    def body(state):
        step, _ = state
        return step + 1, tile(qi - step, False)

    lax.while_loop(cond, body, (jnp.int32(1), tile(qi, True)))
    o_ref[...] = acc_ref[...].astype(BF16)


def _suffix_matrix(blk):
    r = jnp.arange(2 * blk)[:, None] % blk
    c = jnp.arange(2 * blk)[None, :]
    return jnp.where((c >= blk) | (r > c), 1.0, 0.0).astype(BF16)


def _attention(qkv, batch, seq, blk):
    t = qkv.shape[0]
    nq = seq // blk
    return pl.pallas_call(
        functools.partial(_attn_kernel, blk=blk),
        grid=(batch, nq),
        in_specs=[pl.BlockSpec((blk, SB_DIM), lambda b, i: (b * nq + i, 0)),
                  pl.BlockSpec((seq, SB_DIM), lambda b, i: (b, 1)),
                  pl.BlockSpec((seq, SB_DIM), lambda b, i: (b, 2)),
                  pl.BlockSpec((2 * blk, 2 * blk), lambda b, i: (0, 0))],
        out_specs=pl.BlockSpec((blk, SB_DIM), lambda b, i: (b * nq + i, 0)),
        out_shape=jax.ShapeDtypeStruct((t, SB_DIM), BF16),
        scratch_shapes=[pltpu.VMEM((SB_HEADS, blk, blk), F32), pltpu.VMEM((blk, SB_DIM), F32)],
        compiler_params=_params("parallel", "arbitrary"),
        name="stick_breaking_attn",
    )(qkv, qkv, qkv, _suffix_matrix(blk))


def _mix_kernel(x_ref, hc_ref, at_ref, gate_ref, wpw_ref, bpw_ref, wo_ref, wout_ref, bout_ref,
                g_ref, b_ref, x1_ref, x1t_ref):
    d = x_ref.shape[1]
    y_conv = jnp.dot(hc_ref[...], wpw_ref[...], preferred_element_type=F32) + bpw_ref[...]
    y_sb = jnp.dot(at_ref[...], wo_ref[...], preferred_element_type=F32)
    gates = gate_ref[...].astype(F32)
    m = _sigmoid(gates[:, :d]) * y_conv + _sigmoid(gates[:, d:]) * y_sb
    mix = jnp.dot(m.astype(BF16), wout_ref[...], preferred_element_type=F32) + bout_ref[...]
    x1 = _layer_norm(DEEPNORM_ALPHA * x_ref[...] + mix, g_ref[...], b_ref[...])
    x1_ref[...] = x1
    x1t_ref[...] = x1.T.astype(BF16)


def _mix(x2, hc, attn, gates, w_pw2, b_pw2, w_o, w_out, b_out, ln_g, ln_b, tm):
    t, d = x2.shape
    vec = lambda a: a.reshape(1, -1)
    full = lambda a: pl.BlockSpec(a.shape, lambda i: (0, 0))
    rows = lambda w: pl.BlockSpec((tm, w), lambda i: (i, 0))
    args = (x2, hc, attn, gates, w_pw2, vec(b_pw2), w_o, w_out, vec(b_out), vec(ln_g), vec(ln_b))
    return pl.pallas_call(
        _mix_kernel,
        grid=(t // tm,),
        in_specs=[rows(d), rows(CONV_DIM), rows(SB_DIM), rows(2 * d)] + [full(a) for a in args[4:]],
        out_specs=[rows(d), pl.BlockSpec((d, tm), lambda i: (0, i))],
        out_shape=[jax.ShapeDtypeStruct((t, d), F32), jax.ShapeDtypeStruct((d, t), BF16)],
        compiler_params=_params("parallel"),
        name="mix",
    )(*args)


def _top16(s, break_ties):
    n_keys = s.shape[0]
    iota = lax.broadcasted_iota(jnp.int32, s.shape, 0).astype(F32)
    cur = s
    rank = jnp.full(s.shape, float(PEER_TOPK), F32)
    vals = []
    for r in range(PEER_TOPK):
        m = jnp.max(cur, axis=0, keepdims=True)
        sel = cur == m
        if break_ties:
            idx = jnp.min(jnp.where(sel, iota, float(n_keys)), axis=0, keepdims=True)
            sel = iota == idx
        rank = jnp.where(sel, float(r), rank)
        cur = jnp.where(sel, -jnp.inf, cur)
        vals.append(m)
    return jnp.concatenate(vals, axis=0), rank


def _rank_all(scores, v_ref, rank_ref):
    most = None
    for n, s in enumerate(scores):
        v, rank = _top16(s, False)
        v_ref[n] = v
        rank_ref[n] = rank
        ranked = jnp.sum(jnp.where(rank < PEER_TOPK, 1.0, 0.0), axis=0, keepdims=True)
        most = ranked if most is None else jnp.maximum(most, ranked)

    @pl.when(jnp.max(most) > PEER_TOPK)
    def _():
        for n, s in enumerate(scores):
            v, rank = _top16(s, True)
            v_ref[n] = v
            rank_ref[n] = rank


def _staircase(v1, v2):
    k = float(PEER_TOPK)
    iota = lax.broadcasted_iota(jnp.int32, v1.shape, 0).astype(F32)
    top = v1[0:1] + v2[0:1]
    length = jnp.zeros(v1.shape, F32)
    front = v1 + v2[0:1]
    denom = jnp.zeros_like(top)
    for _ in range(PEER_TOPK):
        m = jnp.max(front, axis=0, keepdims=True)
        row = jnp.min(jnp.where(front == m, iota, k), axis=0, keepdims=True)
        sel = iota == row
        denom = denom + jnp.exp(m - top)
        length = jnp.where(sel, length + 1.0, length)
        nxt = jnp.sum(jnp.where(sel, length, 0.0), axis=0, keepdims=True)
        v2_next = jnp.sum(jnp.where(iota == nxt, v2, 0.0), axis=0, keepdims=True)
        v2_next = jnp.where(nxt >= k, -jnp.inf, v2_next)
        front = jnp.where(sel, v1 + v2_next, front)
    return length, denom


def _pack_pair(x):
    bits = pltpu.bitcast(x.astype(BF16).astype(F32), jnp.uint32)
    return bits | (bits >> 16)


def _select_kernel(xt_ref, wq_ref, k1_ref, k2_ref, r2_ref, e2_ref, la_ref, ca_ref, q_ref,
                   v_ref, rank_ref, *, heads_per_step):
    q_ref[...] = jnp.dot(wq_ref[...], xt_ref[...], preferred_element_type=F32).astype(BF16)

    def group(g, _):
        heads = [g * heads_per_step + i for i in range(heads_per_step)]
        scores = []
        for h in heads:
            base = pl.multiple_of(h * 2 * PEER_HALF, 2 * PEER_HALF)
            scores.append(jnp.dot(k1_ref[h], q_ref[pl.ds(base, PEER_HALF), :],
                                  preferred_element_type=F32))
            scores.append(jnp.dot(k2_ref[h], q_ref[pl.ds(base + PEER_HALF, PEER_HALF), :],
                                  preferred_element_type=F32))
        _rank_all(scores, v_ref, rank_ref)
        for i, h in enumerate(heads):
            s1, s2 = scores[2 * i], scores[2 * i + 1]
            v1, rank1 = v_ref[2 * i], rank_ref[2 * i]
            v2, rank2 = v_ref[2 * i + 1], rank_ref[2 * i + 1]
            length, denom = _staircase(v1, v2)
            la = jnp.zeros(s1.shape, F32)
            for r in range(PEER_TOPK):
                la = jnp.where(rank1 == float(r), length[r:r + 1], la)
            r2_ref[h] = rank2.astype(BF16)
            e2_ref[h] = jnp.exp(s2 - v2[0:1]).astype(BF16)
            la_ref[h] = _pack_pair(la)
            ca_ref[h] = _pack_pair(jnp.exp(s1 - v1[0:1]) / denom)
        return 0

    lax.fori_loop(0, PEER_HEADS // heads_per_step, group, 0)


def _select(x1t, wq_t, keys1, keys2, ts):
    d, t = x1t.shape
    nq = wq_t.shape[0]
    out = lambda dt: jax.ShapeDtypeStruct((PEER_HEADS, N_KEYS, t), dt)
    out_spec = pl.BlockSpec((PEER_HEADS, N_KEYS, ts), lambda i: (0, 0, i))
    key_spec = pl.BlockSpec((PEER_HEADS, N_KEYS, PEER_HALF), lambda i: (0, 0, 0))
    heads_per_step = 2
    return pl.pallas_call(
        functools.partial(_select_kernel, heads_per_step=heads_per_step),
        grid=(t // ts,),
        in_specs=[pl.BlockSpec((d, ts), lambda i: (0, i)),
                  pl.BlockSpec((nq, d), lambda i: (0, 0)),
                  key_spec, key_spec],
        out_specs=[out_spec] * 4,
        out_shape=[out(BF16), out(BF16), out(jnp.uint32), out(jnp.uint32)],
        scratch_shapes=[pltpu.VMEM((nq, ts), BF16),
                        pltpu.VMEM((2 * heads_per_step, PEER_TOPK, ts), F32),
                        pltpu.VMEM((2 * heads_per_step, N_KEYS, ts), F32)],
        compiler_params=_params("parallel"),
        name="peer_select",
    )(x1t, wq_t, keys1, keys2)


def _gelu(x):
    return x * (0.5 * (1.0 + lax.erf(x * (1.0 / math.sqrt(2.0)))))


def _row_bf16(ref, h, a, lanes):
    row = ref[h, pl.ds(a, 1), lanes]
    sub = pltpu.bitcast(jnp.broadcast_to(row, (F32_SUBLANES, row.shape[1])), BF16)
    return jnp.concatenate([sub] * (N_KEYS // sub.shape[0]), axis=0)


def _experts_kernel(xt_ref, x1_ref, u_ref, vt_ref, r2_ref, e2_ref, la_ref, ca_ref, g_ref, b_ref,
                    o_ref, acc_ref, w_ref, *, halves, chunk):
    j = pl.program_id(1)
    eb, tb = u_ref.shape[0], xt_ref.shape[1]
    groups = eb // N_KEYS
    sub = eb // halves

    @pl.when(j == 0)
    def _():
        acc_ref[...] = jnp.zeros_like(acc_ref)

    xt = xt_ref[...]
    zero = jnp.zeros((N_KEYS, chunk), BF16)
    for s in range(halves):
        act = jnp.dot(u_ref[s * sub:(s + 1) * sub, :], xt, preferred_element_type=F32)
        for ga in range(sub // N_KEYS):
            a = j * groups + s * (sub // N_KEYS) + ga
            rows = slice(ga * N_KEYS, (ga + 1) * N_KEYS)
            for c in range(tb // chunk):
                lanes = slice(c * chunk, (c + 1) * chunk)
                w = None
                for h in range(PEER_HEADS):
                    keep = r2_ref[h, :, lanes] < _row_bf16(la_ref, h, a, lanes)
                    term = (jnp.where(keep, e2_ref[h, :, lanes], zero)
                            * _row_bf16(ca_ref, h, a, lanes))
                    w = term if w is None else w + term
                w_ref[s * sub + ga * N_KEYS:s * sub + (ga + 1) * N_KEYS, lanes] = (
                    w * _gelu(act[rows, lanes].astype(BF16)))
    acc_ref[...] += jnp.dot(vt_ref[...], w_ref[...], preferred_element_type=F32)

    @pl.when(j == pl.num_programs(1) - 1)
    def _():
        y = acc_ref[...].T
        o_ref[...] = _layer_norm(DEEPNORM_ALPHA * x1_ref[...] + y, g_ref[...], b_ref[...])


def _experts(x1t, x1, u, v_t, sel, ln_g, ln_b, tb, eb):
    d, t = x1t.shape
    n_exp = u.shape[0]
    vec = lambda a: a.reshape(1, -1)
    sel_spec = pl.BlockSpec((PEER_HEADS, N_KEYS, tb), lambda i, j: (0, 0, i))
    return pl.pallas_call(
        functools.partial(_experts_kernel, halves=4, chunk=min(tb, 256)),
        grid=(t // tb, n_exp // eb),
        in_specs=[pl.BlockSpec((d, tb), lambda i, j: (0, i)),
                  pl.BlockSpec((tb, d), lambda i, j: (i, 0)),
                  pl.BlockSpec((eb, d), lambda i, j: (j, 0)),
                  pl.BlockSpec((d, eb), lambda i, j: (0, j)),
                  sel_spec, sel_spec, sel_spec, sel_spec,
                  pl.BlockSpec((1, d), lambda i, j: (0, 0)),
                  pl.BlockSpec((1, d), lambda i, j: (0, 0))],
        out_specs=pl.BlockSpec((tb, d), lambda i, j: (i, 0)),
        out_shape=jax.ShapeDtypeStruct((t, d), F32),
        scratch_shapes=[pltpu.VMEM((d, tb), F32), pltpu.VMEM((eb, tb), BF16)],
        compiler_params=_params("parallel", "arbitrary"),
        name="peer_experts",
    )(x1t, x1, u, v_t, *sel, vec(ln_g), vec(ln_b))


def _tile(total, want):
    return min(total, want)


def _layer(x2, batch, seq, w_in, b_in, conv_w_dw, conv_b_dw, conv_ln_g, conv_ln_b, conv_w_pw2,
           conv_b_pw2, sb_w_o, w_out, b_out, ln1_g, ln1_b, peer_w_q, keys1, keys2, peer_u, peer_v,
           ln2_g, ln2_b):
    t = x2.shape[0]
    hconv, qkv, gates = _in_proj(x2, w_in.astype(BF16), b_in.reshape(1, -1), _tile(t, 512))
    hc = _conv_branch(hconv, conv_w_dw, conv_b_dw, conv_ln_g, conv_ln_b, batch, seq,
                      _tile(seq, 512))
    attn = _attention(qkv, batch, seq, 128)
    x1, x1t = _mix(x2, hc, attn, gates, conv_w_pw2.astype(BF16), conv_b_pw2, sb_w_o.astype(BF16),
                   w_out.astype(BF16), b_out, ln1_g, ln1_b, _tile(t, 512))
    sel = _select(x1t, peer_w_q.T.astype(BF16), keys1.astype(BF16), keys2.astype(BF16),
                  _tile(t, 256))
    return _experts(x1t, x1, peer_u.astype(BF16), peer_v.T.astype(BF16), sel, ln2_g, ln2_b,
                    _tile(t, 512), _tile(peer_u.shape[0], 1024))


def kernel(x, w_in, b_in, conv_w_dw, conv_b_dw, conv_ln_g, conv_ln_b, conv_w_pw2, conv_b_pw2,
           sb_w_o, w_out, b_out, ln1_g, ln1_b, peer_w_q, peer_keys_1, peer_keys_2, peer_u, peer_v,
           ln2_g, ln2_b):
    batch, seq, d = x.shape
    x2 = x.reshape(batch * seq, d)
    for l in range(DEPTH):
        x2 = _layer(x2, batch, seq, w_in[l], b_in[l], conv_w_dw[l], conv_b_dw[l], conv_ln_g[l],
                    conv_ln_b[l], conv_w_pw2[l], conv_b_pw2[l], sb_w_o[l], w_out[l], b_out[l],
                    ln1_g[l], ln1_b[l], peer_w_q[l], peer_keys_1[l], peer_keys_2[l], peer_u[l],
                    peer_v[l], ln2_g[l], ln2_b[l])
    return x2.reshape(batch, seq, d)
```

```python
import functools
import math

import jax
import jax.numpy as jnp
from jax import lax
from jax.experimental import pallas as pl
from jax.experimental.pallas import tpu as pltpu

F32 = jnp.float32
BF16 = jnp.bfloat16

DEPTH = 1
CONV_DIM = 512
CONV_WIDTH = 31
SB_HEADS = 8
SB_HEAD_DIM = 64
SB_DIM = SB_HEADS * SB_HEAD_DIM
PEER_HEADS = 8
PEER_HALF = 128
N_KEYS = 128
PEER_TOPK = 16
LN_EPS = 1e-5
DEEPNORM_ALPHA = (2.0 * DEPTH) ** 0.25
EXP_UNDERFLOW = -104.0

LANES = 128
F32_SUBLANES = 8
HALO = 32
VMEM_LIMIT = 56 * 1024 * 1024


def _params(*sem):
    return pltpu.CompilerParams(dimension_semantics=sem, vmem_limit_bytes=VMEM_LIMIT)


def _layer_norm(x, g, b):
    mu = jnp.mean(x, axis=-1, keepdims=True)
    xc = x - mu
    var = jnp.mean(xc * xc, axis=-1, keepdims=True)
    return xc * lax.rsqrt(var + LN_EPS) * g + b


def _sigmoid(x):
    return 1.0 / (1.0 + jnp.exp(-x))


def _inproj_kernel(x_ref, w_ref, b_ref, conv_ref, qkv_ref, gate_ref, *, chunk):
    xb = x_ref[...].astype(BF16)
    col = 0
    for o_ref in (conv_ref, qkv_ref, gate_ref):
        width = o_ref.shape[1]
        for c in range(0, width, chunk):
            acc = jnp.dot(xb, w_ref[:, col + c:col + c + chunk], preferred_element_type=F32)
            o_ref[:, c:c + chunk] = (acc + b_ref[:, col + c:col + c + chunk]).astype(BF16)
        col += width


def _in_proj(x2, w_in, b_in, tm):
    t, d = x2.shape
    n = w_in.shape[1]
    widths = (2 * CONV_DIM, 3 * SB_DIM, n - 2 * CONV_DIM - 3 * SB_DIM)
    return pl.pallas_call(
        functools.partial(_inproj_kernel, chunk=512),
        grid=(t // tm,),
        in_specs=[pl.BlockSpec((tm, d), lambda i: (i, 0)),
                  pl.BlockSpec((d, n), lambda i: (0, 0)),
                  pl.BlockSpec((1, n), lambda i: (0, 0))],
        out_specs=[pl.BlockSpec((tm, w), lambda i: (i, 0)) for w in widths],
        out_shape=[jax.ShapeDtypeStruct((t, w), BF16) for w in widths],
        compiler_params=_params("parallel"),
        name="in_proj",
    )(x2, w_in, b_in)


def _conv_kernel(cur_ref, halo_ref, wdw_ref, bdw_ref, g_ref, b_ref, o_ref, buf_ref):
    ts = cur_ref.shape[0]

    def glu(ref):
        h = ref[...].astype(F32)
        return h[:, :CONV_DIM] * _sigmoid(h[:, CONV_DIM:])

    first = pl.program_id(1) == 0
    buf_ref[0:HALO, :] = jnp.where(first, 0.0, glu(halo_ref))
    buf_ref[HALO:HALO + ts, :] = glu(cur_ref)
    acc = jnp.zeros((ts, CONV_DIM), F32) + bdw_ref[...]
    for w in range(CONV_WIDTH):
        off = HALO - (CONV_WIDTH - 1) + w
        acc = acc + buf_ref[off:off + ts, :] * wdw_ref[w:w + 1, :]
    y = _layer_norm(acc, g_ref[...], b_ref[...])
    o_ref[...] = (y * _sigmoid(y)).astype(BF16)


def _conv_branch(hconv, w_dw, b_dw, ln_g, ln_b, batch, seq, ts):
    t = hconv.shape[0]
    ns = seq // ts
    per = ts // HALO
    vec = lambda a: a.reshape(1, -1)
    return pl.pallas_call(
        _conv_kernel,
        grid=(batch, ns),
        in_specs=[pl.BlockSpec((ts, 2 * CONV_DIM), lambda b, i: (b * ns + i, 0)),
                  pl.BlockSpec((HALO, 2 * CONV_DIM),
                               lambda b, i: (jnp.maximum((b * ns + i) * per - 1, 0), 0)),
                  pl.BlockSpec((CONV_WIDTH, CONV_DIM), lambda b, i: (0, 0)),
                  pl.BlockSpec((1, CONV_DIM), lambda b, i: (0, 0)),
                  pl.BlockSpec((1, CONV_DIM), lambda b, i: (0, 0)),
                  pl.BlockSpec((1, CONV_DIM), lambda b, i: (0, 0))],
        out_specs=pl.BlockSpec((ts, CONV_DIM), lambda b, i: (b * ns + i, 0)),
        out_shape=jax.ShapeDtypeStruct((t, CONV_DIM), BF16),
        scratch_shapes=[pltpu.VMEM((HALO + ts, CONV_DIM), F32)],
        compiler_params=_params("parallel", "parallel"),
        name="conv_branch",
    )(hconv, hconv, w_dw, vec(b_dw), vec(ln_g), vec(ln_b))


def _attn_kernel(q_ref, k_ref, v_ref, tri_ref, o_ref, carry_ref, acc_ref, *, blk):
    qi = pl.program_id(1)
    scale = 1.0 / math.sqrt(SB_HEAD_DIM)
    row = lax.broadcasted_iota(jnp.int32, (blk, blk), 0)
    col = lax.broadcasted_iota(jnp.int32, (blk, blk), 1)
    causal = col < row
    tri = tri_ref[...]

    def tile(j, diagonal):
        start = pl.multiple_of(j * blk, blk)
        live = None
        for h in range(SB_HEADS):
            lanes = slice(h * SB_HEAD_DIM, (h + 1) * SB_HEAD_DIM)
            z = lax.dot_general(q_ref[:, lanes], k_ref[pl.ds(start, blk), lanes],
                                (((1,), (1,)), ((), ())), preferred_element_type=F32) * scale
            soft = jnp.log(1.0 + jnp.exp(-jnp.abs(z)))
            log_beta = jnp.minimum(z, 0.0) - soft
            log_om = jnp.minimum(-z, 0.0) - soft
            if diagonal:
                log_om = jnp.where(causal, log_om, 0.0)
            hi = log_om.astype(BF16)
            lo = (log_om - hi.astype(F32)).astype(BF16)
            sums = jnp.dot(jnp.concatenate([hi, lo], axis=1), tri, preferred_element_type=F32)
            logit = log_beta + sums[:, :blk]
            if diagonal:
                att = jnp.where(causal, jnp.exp(logit), 0.0)
                carry = sums[:, blk:]
            else:
                att = jnp.exp(logit + carry_ref[h])
                carry = carry_ref[h] + sums[:, blk:]
            carry_ref[h] = carry
            pv = jnp.dot(att.astype(BF16), v_ref[pl.ds(start, blk), lanes],
                         preferred_element_type=F32)
            acc_ref[:, lanes] = pv if diagonal else acc_ref[:, lanes] + pv
            live = carry if live is None else jnp.maximum(live, carry)
        return jnp.max(live)

    def cond(state):
        step, live = state
        return jnp.logical_and(step <= qi, live > EXP_UNDERFLOW)

    def body(state):
        step, _ = state
        return step + 1, tile(qi - step, False)

    lax.while_loop(cond, body, (jnp.int32(1), tile(qi, True)))
    o_ref[...] = acc_ref[...].astype(BF16)


def _suffix_matrix(blk):
    r = jnp.arange(2 * blk)[:, None] % blk
    c = jnp.arange(2 * blk)[None, :]
    return jnp.where((c >= blk) | (r > c), 1.0, 0.0).astype(BF16)


def _attention(qkv, batch, seq, blk):
    t = qkv.shape[0]
    nq = seq // blk
    return pl.pallas_call(
        functools.partial(_attn_kernel, blk=blk),
        grid=(batch, nq),
        in_specs=[pl.BlockSpec((blk, SB_DIM), lambda b, i: (b * nq + i, 0)),
                  pl.BlockSpec((seq, SB_DIM), lambda b, i: (b, 1)),
                  pl.BlockSpec((seq, SB_DIM), lambda b, i: (b, 2)),
                  pl.BlockSpec((2 * blk, 2 * blk), lambda b, i: (0, 0))],
        out_specs=pl.BlockSpec((blk, SB_DIM), lambda b, i: (b * nq + i, 0)),
        out_shape=jax.ShapeDtypeStruct((t, SB_DIM), BF16),
        scratch_shapes=[pltpu.VMEM((SB_HEADS, blk, blk), F32), pltpu.VMEM((blk, SB_DIM), F32)],
        compiler_params=_params("parallel", "arbitrary"),
        name="stick_breaking_attn",
    )(qkv, qkv, qkv, _suffix_matrix(blk))


def _mix_kernel(x_ref, hc_ref, at_ref, gate_ref, wpw_ref, bpw_ref, wo_ref, wout_ref, bout_ref,
                g_ref, b_ref, x1_ref, x1t_ref):
    d = x_ref.shape[1]
    y_conv = jnp.dot(hc_ref[...], wpw_ref[...], preferred_element_type=F32) + bpw_ref[...]
    y_sb = jnp.dot(at_ref[...], wo_ref[...], preferred_element_type=F32)
    gates = gate_ref[...].astype(F32)
    m = _sigmoid(gates[:, :d]) * y_conv + _sigmoid(gates[:, d:]) * y_sb
    mix = jnp.dot(m.astype(BF16), wout_ref[...], preferred_element_type=F32) + bout_ref[...]
    x1 = _layer_norm(DEEPNORM_ALPHA * x_ref[...] + mix, g_ref[...], b_ref[...])
    x1_ref[...] = x1
    x1t_ref[...] = x1.T.astype(BF16)


def _mix(x2, hc, attn, gates, w_pw2, b_pw2, w_o, w_out, b_out, ln_g, ln_b, tm):
    t, d = x2.shape
    vec = lambda a: a.reshape(1, -1)
    full = lambda a: pl.BlockSpec(a.shape, lambda i: (0, 0))
    rows = lambda w: pl.BlockSpec((tm, w), lambda i: (i, 0))
    args = (x2, hc, attn, gates, w_pw2, vec(b_pw2), w_o, w_out, vec(b_out), vec(ln_g), vec(ln_b))
    return pl.pallas_call(
        _mix_kernel,
        grid=(t // tm,),
        in_specs=[rows(d), rows(CONV_DIM), rows(SB_DIM), rows(2 * d)] + [full(a) for a in args[4:]],
        out_specs=[rows(d), pl.BlockSpec((d, tm), lambda i: (0, i))],
        out_shape=[jax.ShapeDtypeStruct((t, d), F32), jax.ShapeDtypeStruct((d, t), BF16)],
        compiler_params=_params("parallel"),
        name="mix",
    )(*args)


def _top16(s, break_ties):
    n_keys = s.shape[0]
    iota = lax.broadcasted_iota(jnp.int32, s.shape, 0).astype(F32)
    cur = s
    rank = jnp.full(s.shape, float(PEER_TOPK), F32)
    vals = []
    for r in range(PEER_TOPK):
        m = jnp.max(cur, axis=0, keepdims=True)
        sel = cur == m
        if break_ties:
            idx = jnp.min(jnp.where(sel, iota, float(n_keys)), axis=0, keepdims=True)
            sel = iota == idx
        rank = jnp.where(sel, float(r), rank)
        cur = jnp.where(sel, -jnp.inf, cur)
        vals.append(m)
    return jnp.concatenate(vals, axis=0), rank


def _rank_all(scores, v_ref, rank_ref):
    most = None
    for n, s in enumerate(scores):
        v, rank = _top16(s, False)
        v_ref[n] = v
        rank_ref[n] = rank
        ranked = jnp.sum(jnp.where(rank < PEER_TOPK, 1.0, 0.0), axis=0, keepdims=True)
        most = ranked if most is None else jnp.maximum(most, ranked)

    @pl.when(jnp.max(most) > PEER_TOPK)
    def _():
        for n, s in enumerate(scores):
            v, rank = _top16(s, True)
            v_ref[n] = v
            rank_ref[n] = rank


def _staircase(v1, v2):
    k = float(PEER_TOPK)
    iota = lax.broadcasted_iota(jnp.int32, v1.shape, 0).astype(F32)
    top = v1[0:1] + v2[0:1]
    length = jnp.zeros(v1.shape, F32)
    front = v1 + v2[0:1]
    denom = jnp.zeros_like(top)
    for _ in range(PEER_TOPK):
        m = jnp.max(front, axis=0, keepdims=True)
        row = jnp.min(jnp.where(front == m, iota, k), axis=0, keepdims=True)
        sel = iota == row
        denom = denom + jnp.exp(m - top)
        length = jnp.where(sel, length + 1.0, length)
        nxt = jnp.sum(jnp.where(sel, length, 0.0), axis=0, keepdims=True)
        v2_next = jnp.sum(jnp.where(iota == nxt, v2, 0.0), axis=0, keepdims=True)
        v2_next = jnp.where(nxt >= k, -jnp.inf, v2_next)
        front = jnp.where(sel, v1 + v2_next, front)
    return length, denom


def _pack_pair(x):
    bits = pltpu.bitcast(x.astype(BF16).astype(F32), jnp.uint32)
    return bits | (bits >> 16)


def _select_kernel(xt_ref, wq_ref, k1_ref, k2_ref, r2_ref, e2_ref, la_ref, ca_ref, q_ref,
                   v_ref, rank_ref, *, heads_per_step):
    q_ref[...] = jnp.dot(wq_ref[...], xt_ref[...], preferred_element_type=F32).astype(BF16)

    def group(g, _):
        heads = [g * heads_per_step + i for i in range(heads_per_step)]
        scores = []
        for h in heads:
            base = pl.multiple_of(h * 2 * PEER_HALF, 2 * PEER_HALF)
            scores.append(jnp.dot(k1_ref[h], q_ref[pl.ds(base, PEER_HALF), :],
                                  preferred_element_type=F32))
            scores.append(jnp.dot(k2_ref[h], q_ref[pl.ds(base + PEER_HALF, PEER_HALF), :],
                                  preferred_element_type=F32))
        _rank_all(scores, v_ref, rank_ref)
        for i, h in enumerate(heads):
            s1, s2 = scores[2 * i], scores[2 * i + 1]
            v1, rank1 = v_ref[2 * i], rank_ref[2 * i]
            v2, rank2 = v_ref[2 * i + 1], rank_ref[2 * i + 1]
            length, denom = _staircase(v1, v2)
            la = jnp.zeros(s1.shape, F32)
            for r in range(PEER_TOPK):
                la = jnp.where(rank1 == float(r), length[r:r + 1], la)
            r2_ref[h] = rank2.astype(BF16)
            e2_ref[h] = jnp.exp(s2 - v2[0:1]).astype(BF16)
            la_ref[h] = _pack_pair(la)
            ca_ref[h] = _pack_pair(jnp.exp(s1 - v1[0:1]) / denom)
        return 0

    lax.fori_loop(0, PEER_HEADS // heads_per_step, group, 0)


def _select(x1t, wq_t, keys1, keys2, ts):
    d, t = x1t.shape
    nq = wq_t.shape[0]
    out = lambda dt: jax.ShapeDtypeStruct((PEER_HEADS, N_KEYS, t), dt)
    out_spec = pl.BlockSpec((PEER_HEADS, N_KEYS, ts), lambda i: (0, 0, i))
    key_spec = pl.BlockSpec((PEER_HEADS, N_KEYS, PEER_HALF), lambda i: (0, 0, 0))
    heads_per_step = 2
    return pl.pallas_call(
        functools.partial(_select_kernel, heads_per_step=heads_per_step),
        grid=(t // ts,),
        in_specs=[pl.BlockSpec((d, ts), lambda i: (0, i)),
                  pl.BlockSpec((nq, d), lambda i: (0, 0)),
                  key_spec, key_spec],
        out_specs=[out_spec] * 4,
        out_shape=[out(BF16), out(BF16), out(jnp.uint32), out(jnp.uint32)],
        scratch_shapes=[pltpu.VMEM((nq, ts), BF16),
                        pltpu.VMEM((2 * heads_per_step, PEER_TOPK, ts), F32),
                        pltpu.VMEM((2 * heads_per_step, N_KEYS, ts), F32)],
        compiler_params=_params("parallel"),
        name="peer_select",
    )(x1t, wq_t, keys1, keys2)


def _gelu(x):
    return x * (0.5 * (1.0 + lax.erf(x * (1.0 / math.sqrt(2.0)))))


def _row_bf16(ref, h, a, lanes):
    row = ref[h, pl.ds(a, 1), lanes]
    sub = pltpu.bitcast(jnp.broadcast_to(row, (F32_SUBLANES, row.shape[1])), BF16)
    return jnp.concatenate([sub] * (N_KEYS // sub.shape[0]), axis=0)


def _experts_kernel(xt_ref, x1_ref, u_ref, vt_ref, r2_ref, e2_ref, la_ref, ca_ref, g_ref, b_ref,
                    o_ref, acc_ref, w_ref, *, halves, chunk):
    j = pl.program_id(1)
    eb, tb = u_ref.shape[0], xt_ref.shape[1]
    groups = eb // N_KEYS
    sub = eb // halves

    @pl.when(j == 0)
    def _():
        acc_ref[...] = jnp.zeros_like(acc_ref)

    xt = xt_ref[...]
    zero = jnp.zeros((N_KEYS, chunk), BF16)
    for s in range(halves):
        act = jnp.dot(u_ref[s * sub:(s + 1) * sub, :], xt, preferred_element_type=F32)
        for ga in range(sub // N_KEYS):
            a = j * groups + s * (sub // N_KEYS) + ga
            rows = slice(ga * N_KEYS, (ga + 1) * N_KEYS)
            for c in range(tb // chunk):
                lanes = slice(c * chunk, (c + 1) * chunk)
                w = None
                for h in range(PEER_HEADS):
                    keep = r2_ref[h, :, lanes] < _row_bf16(la_ref, h, a, lanes)
                    term = (jnp.where(keep, e2_ref[h, :, lanes], zero)
                            * _row_bf16(ca_ref, h, a, lanes))
                    w = term if w is None else w + term
                w_ref[s * sub + ga * N_KEYS:s * sub + (ga + 1) * N_KEYS, lanes] = (
                    w * _gelu(act[rows, lanes].astype(BF16)))
    acc_ref[...] += jnp.dot(vt_ref[...], w_ref[...], preferred_element_type=F32)

    @pl.when(j == pl.num_programs(1) - 1)
    def _():
        y = acc_ref[...].T
        o_ref[...] = _layer_norm(DEEPNORM_ALPHA * x1_ref[...] + y, g_ref[...], b_ref[...])


def _experts(x1t, x1, u, v_t, sel, ln_g, ln_b, tb, eb):
    d, t = x1t.shape
    n_exp = u.shape[0]
    vec = lambda a: a.reshape(1, -1)
    sel_spec = pl.BlockSpec((PEER_HEADS, N_KEYS, tb), lambda i, j: (0, 0, i))
    return pl.pallas_call(
        functools.partial(_experts_kernel, halves=2, chunk=min(tb, 256)),
        grid=(t // tb, n_exp // eb),
        in_specs=[pl.BlockSpec((d, tb), lambda i, j: (0, i)),
                  pl.BlockSpec((tb, d), lambda i, j: (i, 0)),
                  pl.BlockSpec((eb, d), lambda i, j: (j, 0)),
                  pl.BlockSpec((d, eb), lambda i, j: (0, j)),
                  sel_spec, sel_spec, sel_spec, sel_spec,
                  pl.BlockSpec((1, d), lambda i, j: (0, 0)),
                  pl.BlockSpec((1, d), lambda i, j: (0, 0))],
        out_specs=pl.BlockSpec((tb, d), lambda i, j: (i, 0)),
        out_shape=jax.ShapeDtypeStruct((t, d), F32),
        scratch_shapes=[pltpu.VMEM((d, tb), F32), pltpu.VMEM((eb, tb), BF16)],
        compiler_params=_params("parallel", "arbitrary"),
        name="peer_experts",
    )(x1t, x1, u, v_t, *sel, vec(ln_g), vec(ln_b))


def _tile(total, want):
    return min(total, want)


def _layer(x2, batch, seq, w_in, b_in, conv_w_dw, conv_b_dw, conv_ln_g, conv_ln_b, conv_w_pw2,
           conv_b_pw2, sb_w_o, w_out, b_out, ln1_g, ln1_b, peer_w_q, keys1, keys2, peer_u, peer_v,
           ln2_g, ln2_b):
    t = x2.shape[0]
    hconv, qkv, gates = _in_proj(x2, w_in.astype(BF16), b_in.reshape(1, -1), _tile(t, 512))
    hc = _conv_branch(hconv, conv_w_dw, conv_b_dw, conv_ln_g, conv_ln_b, batch, seq,
                      _tile(seq, 512))
    attn = _attention(qkv, batch, seq, _tile(seq, 256))
    x1, x1t = _mix(x2, hc, attn, gates, conv_w_pw2.astype(BF16), conv_b_pw2, sb_w_o.astype(BF16),
                   w_out.astype(BF16), b_out, ln1_g, ln1_b, _tile(t, 1024))
    sel = _select(x1t, peer_w_q.T.astype(BF16), keys1.astype(BF16), keys2.astype(BF16),
                  _tile(t, 512))
    return _experts(x1t, x1, peer_u.astype(BF16), peer_v.T.astype(BF16), sel, ln2_g, ln2_b,
                    _tile(t, 512), _tile(peer_u.shape[0], 2048))


def kernel(x, w_in, b_in, conv_w_dw, conv_b_dw, conv_ln_g, conv_ln_b, conv_w_pw2, conv_b_pw2,
           sb_w_o, w_out, b_out, ln1_g, ln1_b, peer_w_q, peer_keys_1, peer_keys_2, peer_u, peer_v,
           ln2_g, ln2_b):
    batch, seq, d = x.shape
    x2 = x.reshape(batch * seq, d)
    for l in range(DEPTH):
        x2 = _layer(x2, batch, seq, w_in[l], b_in[l], conv_w_dw[l], conv_b_dw[l], conv_ln_g[l],
                    conv_ln_b[l], conv_w_pw2[l], conv_b_pw2[l], sb_w_o[l], w_out[l], b_out[l],
                    ln1_g[l], ln1_b[l], peer_w_q[l], peer_keys_1[l], peer_keys_2[l], peer_u[l],
                    peer_v[l], ln2_g[l], ln2_b[l])
    return x2.reshape(batch, seq, d)
```
